```python
import jax
import jax.numpy as jnp
from jax import lax
import numpy as np

D_MODEL = 1024
BATCH = 8
SEQ = 4096
DEPTH = 4

GRID_W = 64
CTX_LEN = 256
EPS = 1e-6
N_EVEN = (DEPTH + 1) // 2
N_ODD = DEPTH // 2

NA_HEADS = 8
NA_DH = 64
NA_W = NA_HEADS * NA_DH
NA_WIN_R = 8
NA_WIN_C = 16
ML_HEADS = 4
ML_DQK = 64
ML_DV = 128
ML_QK_W = ML_HEADS * ML_DQK
ML_V_W = ML_HEADS * ML_DV
ML_CHUNK = 64
ROPE_THETA = 10000.0
EVEN_IN = 3 * NA_W + 2 * ML_QK_W + 2 * ML_V_W + 4 * ML_HEADS
EVEN_MIX = NA_W + ML_V_W
DN_HEADS = 8
DN_DK = 128
DN_DV = 128
DN_QK_W = DN_HEADS * DN_DK
DN_V_W = DN_HEADS * DN_DV
DN_CHUNK = 64
DN_CONV = 5
DN_CONV_CH = 2 * DN_QK_W + DN_V_W
ODD_IN = 2 * DN_QK_W + 2 * DN_V_W + 4 * DN_HEADS
N_EXPERTS = 32
TOP_K = 4
D_EXPERT = 1024
SWIGLU_LIMIT = 7.0
SWIGLU_ALPHA = 1.702
MOE_BLOCK = 256

kernel_name = 'hybrid_na_mlstm_gdn_moe_diffusion_trunk'

F32 = jnp.float32


def heads(a, n):
    return a.reshape(a.shape[:-1] + (n, a.shape[-1] // n))


def rms_norm(x, gain):
    xf = x.astype(F32)
    y = xf * lax.rsqrt(jnp.mean(xf * xf, axis=-1, keepdims=True) + EPS)
    return (y * gain.astype(F32)).astype(x.dtype)


def l2norm(x):
    return x * lax.rsqrt(jnp.sum(x * x, axis=-1, keepdims=True) + EPS)


def adaln(cond, w, b):
    return jax.nn.silu(cond) @ w + b


def modulate(h, shift, scale):
    return h * (1.0 + scale) + shift


def axial_rope(n_tokens):
    t = jnp.arange(n_tokens)
    n_freq = ML_DQK // 4
    inv = ROPE_THETA ** (-jnp.arange(n_freq, dtype=F32) / n_freq)
    ang = jnp.concatenate([(t // GRID_W).astype(F32)[:, None] * inv,
                           (t % GRID_W).astype(F32)[:, None] * inv], axis=-1)
    return jnp.cos(ang)[:, None, :], jnp.sin(ang)[:, None, :]


def apply_rope(x, cos, sin):
    half = x.shape[-1] // 2
    x1, x2 = x[..., :half], x[..., half:]
    return jnp.concatenate([x1 * cos - x2 * sin, x2 * cos + x1 * sin], axis=-1)


def short_conv(x, w):
    ch, width = w.shape
    kern = w.T[:, None, :].astype(x.dtype)
    return lax.conv_general_dilated(x, kern, window_strides=(1,), padding=[(width // 2, width // 2)],
                                    dimension_numbers=('NWC', 'WIO', 'NWC'), feature_group_count=ch)


def neighbourhood_attention(q, k, v, kc, vc, rpb):
    b, s, h, dh = q.shape
    rows = s // GRID_W
    win_r = min(NA_WIN_R, rows)
    n_loc = win_r * NA_WIN_C

    def grid(a):
        return a.reshape(b, rows, GRID_W, h, dh).transpose(0, 3, 1, 2, 4)

    qg, kg, vg = grid(q), grid(k), grid(v)
    col_start = np.clip(np.arange(GRID_W) - NA_WIN_C // 2, 0, GRID_W - NA_WIN_C)
    col_idx = col_start[:, None] + np.arange(NA_WIN_C)[None, :]
    dcol_idx = col_idx - np.arange(GRID_W)[:, None] + NA_WIN_C - 1
    scale = dh ** -0.5
    rpb32 = rpb.astype(F32)

    def one_row(r):
        r0 = jnp.clip(r - win_r // 2, 0, rows - win_r)
        k_win = lax.dynamic_slice_in_dim(kg, r0, win_r, axis=2)[:, :, :, col_idx]
        v_win = lax.dynamic_slice_in_dim(vg, r0, win_r, axis=2)[:, :, :, col_idx]
        q_row = lax.dynamic_index_in_dim(qg, r, axis=2, keepdims=False)
        drow_idx = r0 + jnp.arange(win_r) - r + NA_WIN_R - 1
        bias = rpb32[:, drow_idx[None, :, None], dcol_idx[:, None, :]]
        s_loc = jnp.einsum('bhcd,bhicjd->bhcij', q_row, k_win).astype(F32) * scale + bias
        s_ctx = jnp.einsum('bhcd,bhld->bhcl', q_row, kc).astype(F32) * scale
        p = jax.nn.softmax(jnp.concatenate([s_loc.reshape(b, h, GRID_W, n_loc), s_ctx], axis=-1), axis=-1)
        p = p.astype(v.dtype)
        p_loc = p[..., :n_loc].reshape(b, h, GRID_W, win_r, NA_WIN_C)
        return (jnp.einsum('bhcij,bhicjd->bhcd', p_loc, v_win)
                + jnp.einsum('bhcl,bhld->bhcd', p[..., n_loc:], vc))

    out = lax.map(one_row, jnp.arange(rows))
    return out.transpose(1, 0, 3, 2, 4).reshape(b, s, h * dh)


def context_attention(qc, kc, vc):
    b, h, l, dh = qc.shape
    s = jnp.einsum('bhqd,bhkd->bhqk', qc, kc).astype(F32) * dh ** -0.5
    p = jax.nn.softmax(s, axis=-1).astype(vc.dtype)
    o = jnp.einsum('bhqk,bhkd->bhqd', p, vc)
    return o.transpose(0, 2, 1, 3).reshape(b, l, h * dh)


def mlstm_chunk_scan(q, k, v, li, lf, state):
    b, h, t, _ = q.shape
    n_chunks = t // ML_CHUNK

    def chunks(a):
        return jnp.moveaxis(a.reshape((b, h, n_chunks, ML_CHUNK) + a.shape[3:]), 2, 0)

    lower = jnp.tril(jnp.ones((ML_CHUNK, ML_CHUNK), bool))

    def step(carry, inp):
        c_mat, n_vec, m = carry
        qc, kc, vc, lic, lfc = inp
        cum = jnp.cumsum(lfc, axis=-1)
        log_d = jnp.where(lower, cum[..., :, None] - cum[..., None, :] + lic[..., None, :], -jnp.inf)
        m_inter = cum + m[..., None]
        m_t = jnp.maximum(jnp.max(log_d, axis=-1), m_inter)
        s = jnp.einsum('bhtd,bhsd->bhts', qc, kc) * jnp.exp(log_d - m_t[..., None])
        w_inter = jnp.exp(m_inter - m_t)
        num = s @ vc + w_inter[..., None] * jnp.einsum('bhtd,bhdv->bhtv', qc, c_mat)
        den = jnp.sum(s, axis=-1) + w_inter * jnp.einsum('bhtd,bhd->bht', qc, n_vec)
        h_out = num / jnp.maximum(jnp.abs(den), jnp.exp(-m_t))[..., None]
        cum_last = cum[..., -1]
        log_w = cum_last[..., None] - cum + lic
        m_new = jnp.maximum(cum_last + m, jnp.max(log_w, axis=-1))
        w = jnp.exp(log_w - m_new[..., None])
        decay = jnp.exp(cum_last + m - m_new)
        c_mat = decay[..., None, None] * c_mat + jnp.einsum('bhs,bhsd,bhsv->bhdv', w, kc, vc)
        n_vec = decay[..., None] * n_vec + jnp.einsum('bhs,bhsd->bhd', w, kc)
        return (c_mat, n_vec, m_new), h_out

    state, hs = lax.scan(step, state, tuple(chunks(a) for a in (q, k, v, li, lf)))
    return jnp.moveaxis(hs, 0, 2).reshape(b, h, t, v.shape[-1]), state


def gdn_chunk_scan(q, k, v, g, beta, state):
    b, h, t, dk = q.shape
    n_chunks = t // DN_CHUNK

    def chunks(a):
        return a.reshape((b, h, n_chunks, DN_CHUNK) + a.shape[3:])

    q, k, v, g, beta = (chunks(a) for a in (q, k, v, g, beta))
    incl = jnp.tril(jnp.ones((DN_CHUNK, DN_CHUNK), bool))
    strict = jnp.tril(jnp.ones((DN_CHUNK, DN_CHUNK), bool), -1)
    gc = jnp.cumsum(g, axis=-1)
    decay = jnp.exp(jnp.where(incl, gc[..., :, None] - gc[..., None, :], -jnp.inf))
    kb = k * beta[..., None]
    a_mat = jnp.where(strict, jnp.einsum('bhnid,bhnjd->bhnij', kb, k) * decay, 0.0)
    eye = jnp.eye(DN_CHUNK, dtype=F32)
    t_inv = lax.linalg.triangular_solve(eye + a_mat, jnp.broadcast_to(eye, a_mat.shape),
                                        left_side=True, lower=True, unit_diagonal=True)
    u = t_inv @ (v * beta[..., None])
    w = t_inv @ (kb * jnp.exp(gc)[..., None])
    attn = jnp.einsum('bhnid,bhnjd->bhnij', q, k) * decay

    def step(s_mat, inp):
        qi, ki, ui, wi, gi, ai = inp
        v_new = ui - wi @ s_mat
        o = (qi * jnp.exp(gi)[..., None]) @ s_mat + ai @ v_new
        g_last = gi[..., -1]
        s_mat = (s_mat * jnp.exp(g_last)[..., None, None]
                 + jnp.einsum('bhsd,bhsv->bhdv', ki * jnp.exp(g_last[..., None] - gi)[..., None], v_new))
        return s_mat, o

    state, outs = lax.scan(step, state, tuple(jnp.moveaxis(a, 2, 0) for a in (q, k, u, w, gc, attn)))
    return jnp.moveaxis(outs, 0, 2).reshape(b, h, t, v.shape[-1]), state


def flip_t(seq):
    return tuple(jnp.flip(a, axis=2) for a in seq)


def ml_prep(parts, gate_b, rope):
    q, k, v, o, g = parts
    b, t = q.shape[:2]
    q = heads(q, ML_HEADS).astype(F32)
    k = heads(k, ML_HEADS).astype(F32) * ML_DQK ** -0.5
    v = heads(v, ML_HEADS).astype(F32)
    if rope is not None:
        q = apply_rope(q, *rope)
        k = apply_rope(k, *rope)
    g = (g.astype(F32).reshape(b, t, 4, ML_HEADS) + gate_b.astype(F32)).transpose(2, 0, 3, 1)
    tr = lambda a: a.transpose(0, 2, 1, 3)
    return tr(q), tr(k), tr(v), g, o


def mlstm_bidirectional(lat, ctx, out_gain):
    ql, kl, vl, gl, ol = lat
    qc, kc, vc, gc, oc = ctx
    b, h = ql.shape[:2]
    zero = (jnp.zeros((b, h, ML_DQK, ML_DV), F32), jnp.zeros((b, h, ML_DQK), F32), jnp.zeros((b, h), F32))
    h_l, h_c = [], []
    for d in range(2):
        seq_c = (qc, kc, vc, gc[2 * d], jax.nn.log_sigmoid(gc[2 * d + 1]))
        seq_l = (ql, kl, vl, gl[2 * d], jax.nn.log_sigmoid(gl[2 * d + 1]))
        if d == 1:
            seq_c, seq_l = flip_t(seq_c), flip_t(seq_l)
        hc_d, ctx_state = mlstm_chunk_scan(*seq_c, zero)
        hl_d, _ = mlstm_chunk_scan(*seq_l, ctx_state)
        if d == 1:
            hc_d, hl_d = jnp.flip(hc_d, axis=2), jnp.flip(hl_d, axis=2)
        h_c.append(hc_d)
        h_l.append(hl_d)

    def finish(hh, o):
        hh = rms_norm(hh, out_gain)
        bb, hd, t, dv = hh.shape
        return hh.transpose(0, 2, 1, 3).reshape(bb, t, hd * dv).astype(o.dtype) * jax.nn.sigmoid(o)

    return finish(h_l[0] + h_l[1], ol), finish(h_c[0] + h_c[1], oc)


def even_mixer(hl, hc, w_in, w_out, q_gain, k_gain, rpb, gate_b, out_gain, rope, need_ctx):
    split_at = np.cumsum([NA_W, NA_W, NA_W, ML_QK_W, ML_QK_W, ML_V_W, ML_V_W])
    pl = jnp.split(hl @ w_in, split_at, axis=-1)
    pc = jnp.split(hc @ w_in, split_at, axis=-1)

    def na_qkv(p):
        return (rms_norm(heads(p[0], NA_HEADS), q_gain), rms_norm(heads(p[1], NA_HEADS), k_gain),
                heads(p[2], NA_HEADS))

    ql, kl, vl = na_qkv(pl)
    qc, kc, vc = (a.transpose(0, 2, 1, 3) for a in na_qkv(pc))
    a_l = neighbourhood_attention(ql, kl, vl, kc, vc, rpb)
    b_l, b_c = mlstm_bidirectional(ml_prep(pl[3:], gate_b, rope), ml_prep(pc[3:], gate_b, None), out_gain)
    y_l = jnp.concatenate([a_l, b_l.astype(a_l.dtype)], axis=-1) @ w_out
    if not need_ctx:
        return y_l, None
    a_c = context_attention(qc, kc, vc)
    y_c = jnp.concatenate([a_c, b_c.astype(a_c.dtype)], axis=-1) @ w_out
    return y_l, y_c


def dn_prep(h, w_in, conv_w, a_log, dt_bias):
    b, t, _ = h.shape
    qkv, z, a, bt = jnp.split(h @ w_in, [DN_CONV_CH, DN_CONV_CH + DN_V_W, DN_CONV_CH + DN_V_W + 2 * DN_HEADS], axis=-1)
    qkv = jax.nn.silu(short_conv(qkv, conv_w))
    q, k, v = jnp.split(qkv, [DN_QK_W, 2 * DN_QK_W], axis=-1)
    q = l2norm(heads(q, DN_HEADS).astype(F32)) * DN_DK ** -0.5
    k = l2norm(heads(k, DN_HEADS).astype(F32))
    v = heads(v, DN_HEADS).astype(F32)
    a = a.astype(F32).reshape(b, t, 2, DN_HEADS)
    bt = bt.astype(F32).reshape(b, t, 2, DN_HEADS)
    g = -jnp.exp(a_log.astype(F32)) * jax.nn.softplus(a + dt_bias.astype(F32))
    beta = jax.nn.sigmoid(bt)
    tr = lambda x_: x_.transpose(0, 2, 1, 3)
    return tr(q), tr(k), tr(v), g.transpose(2, 0, 3, 1), beta.transpose(2, 0, 3, 1), heads(z, DN_HEADS)


def odd_mixer(hl, hc, w_in, w_out, conv_w, a_log, dt_bias, out_gain, need_ctx):
    ql, kl, vl, gl, bl, zl = dn_prep(hl, w_in, conv_w, a_log, dt_bias)
    qc, kc, vc, gc, bc, zc = dn_prep(hc, w_in, conv_w, a_log, dt_bias)
    zero = jnp.zeros((hl.shape[0], DN_HEADS, DN_DK, DN_DV), F32)
    o_l, o_c = [], []
    for d in range(2):
        seq_c = (qc, kc, vc, gc[d], bc[d])
        seq_l = (ql, kl, vl, gl[d], bl[d])
        if d == 1:
            seq_c, seq_l = flip_t(seq_c), flip_t(seq_l)
        oc_d, ctx_state = gdn_chunk_scan(*seq_c, zero)
        ol_d, _ = gdn_chunk_scan(*seq_l, ctx_state)
        if d == 1:
            oc_d, ol_d = jnp.flip(oc_d, axis=2), jnp.flip(ol_d, axis=2)
        o_c.append(oc_d)
        o_l.append(ol_d)

    def finish(o, z):
        o = o.transpose(0, 2, 1, 3)
        y = rms_norm(o, out_gain) * jax.nn.silu(z.astype(F32))
        return y.reshape(o.shape[0], o.shape[1], DN_V_W).astype(hl.dtype) @ w_out

    y_l = finish(o_l[0] + o_l[1], zl)
    return y_l, (finish(o_c[0] + o_c[1], zc) if need_ctx else None)


def clamped_swiglu(h):
    glu, lin = h[..., ::2], h[..., 1::2]
    glu = jnp.minimum(glu, SWIGLU_LIMIT)
    lin = jnp.clip(lin, -SWIGLU_LIMIT, SWIGLU_LIMIT)
    return glu * jax.nn.sigmoid(SWIGLU_ALPHA * glu) * (lin + 1.0)


def moe(tokens, router_w, router_b, w1, b1, w2, b2):
    n_tok, d = tokens.shape
    logits = (tokens @ router_w + router_b).astype(F32)
    top_val, top_exp = lax.top_k(logits, TOP_K)
    gate = jax.nn.softmax(top_val, axis=-1)
    n_asg = n_tok * TOP_K
    flat_exp = top_exp.reshape(-1)
    order = jnp.argsort(flat_exp)
    exp_sorted = flat_exp[order]
    tok_sorted = (jnp.arange(n_asg, dtype=jnp.int32) // TOP_K)[order]
    gate_sorted = gate.reshape(-1)[order]
    counts = jnp.bincount(flat_exp, length=N_EXPERTS)
    padded = (counts + MOE_BLOCK - 1) // MOE_BLOCK * MOE_BLOCK
    pad_end = jnp.cumsum(padded)
    dest = (pad_end - padded)[exp_sorted] + jnp.arange(n_asg) - (jnp.cumsum(counts) - counts)[exp_sorted]
    n_blocks = -(-n_asg // MOE_BLOCK) + N_EXPERTS
    n_slots = n_blocks * MOE_BLOCK
    slot_tok = jnp.full((n_slots,), n_tok, jnp.int32).at[dest].set(tok_sorted)
    slot_gate = jnp.zeros((n_slots,), F32).at[dest].set(gate_sorted)
    block_exp = jnp.minimum(jnp.searchsorted(pad_end, jnp.arange(n_blocks) * MOE_BLOCK, side='right'), N_EXPERTS - 1)
    tok_pad = jnp.concatenate([tokens, jnp.zeros((1, d), tokens.dtype)], axis=0)

    def expert_block(args):
        idx, e = args
        hid = tok_pad[idx] @ w1[e] + b1[e]
        return clamped_swiglu(hid) @ w2[e] + b2[e]

    y = lax.map(expert_block, (slot_tok.reshape(n_blocks, MOE_BLOCK), block_exp))
    y = y.reshape(n_slots, d) * slot_gate[:, None].astype(y.dtype)
    return jnp.zeros((n_tok + 1, d), y.dtype).at[slot_tok].add(y)[:n_tok]


def setup_inputs(seed: int = 0) -> dict:
    key = jax.random.key(seed)
    ks = iter(jax.random.split(key, 40))
    nrm = lambda shape, scale: jax.random.normal(next(ks), shape, F32) * scale
    d = D_MODEL
    dt = jnp.exp(jax.random.uniform(next(ks), (N_ODD, 2, DN_HEADS), F32, np.log(1e-3), np.log(1e-1)))
    return {
        'x': nrm((BATCH, SEQ, d), 1.0),
        'c': nrm((BATCH, d), 1.0),
        'ctx': nrm((BATCH, CTX_LEN, d), 1.0),
        'c_ctx': nrm((d,), 1.0),
        'ada_w': nrm((DEPTH, d, 6 * d), 0.5 * d ** -0.5),
        'ada_b': nrm((DEPTH, 6 * d), 0.02),
        'norm1_g': 1.0 + nrm((DEPTH, d), 0.1),
        'norm2_g': 1.0 + nrm((DEPTH, d), 0.1),
        'ev_w_in': nrm((N_EVEN, d, EVEN_IN), d ** -0.5),
        'ev_w_out': nrm((N_EVEN, EVEN_MIX, d), EVEN_MIX ** -0.5),
        'na_q_gain': 1.0 + nrm((N_EVEN, NA_DH), 0.1),
        'na_k_gain': 1.0 + nrm((N_EVEN, NA_DH), 0.1),
        'na_rpb': nrm((N_EVEN, NA_HEADS, 2 * NA_WIN_R - 1, 2 * NA_WIN_C - 1), 0.1),
        'ml_gate_b': jnp.array([0.0, 3.0, 0.0, 3.0], F32)[None, :, None] + nrm((N_EVEN, 4, ML_HEADS), 0.1),
        'ml_out_gain': 1.0 + nrm((N_EVEN, ML_DV), 0.1),
        'od_w_in': nrm((N_ODD, d, ODD_IN), d ** -0.5),
        'od_w_out': nrm((N_ODD, DN_V_W, d), DN_V_W ** -0.5),
        'dn_conv_w': nrm((N_ODD, DN_CONV_CH, DN_CONV), DN_CONV ** -0.5),
        'dn_a_log': jnp.log(jax.random.uniform(next(ks), (N_ODD, 2, DN_HEADS), F32, 1.0, 16.0)),
        'dn_dt_bias': dt + jnp.log(-jnp.expm1(-dt)),
        'dn_out_gain': 1.0 + nrm((N_ODD, DN_DV), 0.1),
        'router_w': nrm((DEPTH, d, N_EXPERTS), d ** -0.5),
        'router_b': nrm((DEPTH, N_EXPERTS), 0.01),
        'exp_w1': nrm((DEPTH, N_EXPERTS, d, 2 * D_EXPERT), d ** -0.5),
        'exp_b1': nrm((DEPTH, N_EXPERTS, 2 * D_EXPERT), 0.02),
        'exp_w2': nrm((DEPTH, N_EXPERTS, D_EXPERT, d), D_EXPERT ** -0.5),
        'exp_b2': nrm((DEPTH, N_EXPERTS, d), 0.02),
    }


def reference(x, c, ctx, c_ctx, ada_w, ada_b, norm1_g, norm2_g, ev_w_in, ev_w_out, na_q_gain, na_k_gain,
              na_rpb, ml_gate_b, ml_out_gain, od_w_in, od_w_out, dn_conv_w, dn_a_log, dn_dt_bias,
              dn_out_gain, router_w, router_b, exp_w1, exp_b1, exp_w2, exp_b2):
    b, s, d = x.shape
    l = ctx.shape[1]
    rope = axial_rope(s)
    xl, xc = x, ctx
    for layer in range(DEPTH):
        need_ctx = layer < DEPTH - 1
        mod_l = adaln(c, ada_w[layer], ada_b[layer])[:, None, :]
        mod_c = adaln(c_ctx[None], ada_w[layer], ada_b[layer])[:, None, :]
        sh1l, sc1l, g1l, sh2l, sc2l, g2l = jnp.split(mod_l, 6, axis=-1)
        sh1c, sc1c, g1c, sh2c, sc2c, g2c = jnp.split(mod_c, 6, axis=-1)
        hl = modulate(rms_norm(xl, norm1_g[layer]), sh1l, sc1l)
        hc = modulate(rms_norm(xc, norm1_g[layer]), sh1c, sc1c)
        if layer % 2 == 0:
            e = layer // 2
            yl, yc = even_mixer(hl, hc, ev_w_in[e], ev_w_out[e], na_q_gain[e], na_k_gain[e], na_rpb[e],
                                ml_gate_b[e], ml_out_gain[e], rope, need_ctx)
        else:
            o = layer // 2
            yl, yc = odd_mixer(hl, hc, od_w_in[o], od_w_out[o], dn_conv_w[o], dn_a_log[o], dn_dt_bias[o],
                               dn_out_gain[o], need_ctx)
        xl = xl + g1l * yl
        hl = modulate(rms_norm(xl, norm2_g[layer]), sh2l, sc2l)
        moe_p = (router_w[layer], router_b[layer], exp_w1[layer], exp_b1[layer], exp_w2[layer], exp_b2[layer])
        if need_ctx:
            xc = xc + g1c * yc
            hc = modulate(rms_norm(xc, norm2_g[layer]), sh2c, sc2c)
            f = moe(jnp.concatenate([hl.reshape(-1, d), hc.reshape(-1, d)], axis=0), *moe_p)
            fl = f[:b * s].reshape(b, s, d)
            xc = xc + g2c * f[b * s:].reshape(b, l, d)
        else:
            fl = moe(hl.reshape(-1, d), *moe_p).reshape(b, s, d)
        xl = xl + g2l * fl
    return xl
```

```python
import functools

import numpy as np
import jax
import jax.numpy as jnp
from jax import lax
from jax.experimental import pallas as pl
from jax.experimental.pallas import tpu as pltpu

F32 = jnp.float32
MXU_DT = jnp.bfloat16
EPS = 1e-6
NEG = -1e30

D_MODEL = 1024
GRID_W = 64
CTX_LEN = 256
TM = 256
LANES = 128

NA_HEADS, NA_DH = 8, 64
NA_W = NA_HEADS * NA_DH
NA_WIN_R, NA_WIN_C = 8, 16
NA_RB = 4
NA_QB = NA_RB * GRID_W
NA_KROWS = NA_RB + NA_WIN_R - 1
NA_KU = NA_KROWS * GRID_W

ML_HEADS, ML_DQK, ML_DV, ML_CHUNK = 4, 64, 128, 64
ML_QK_W, ML_V_W = ML_HEADS * ML_DQK, ML_HEADS * ML_DV
ROPE_THETA = 10000.0
EVEN_IN = 3 * NA_W + 2 * ML_QK_W + 2 * ML_V_W + 4 * ML_HEADS
EVEN_PAD = 3200
E_Q, E_K, E_V = 0, NA_W, 2 * NA_W
E_MQ = 3 * NA_W
E_MK = E_MQ + ML_QK_W
E_MV = E_MK + ML_QK_W
E_MO = E_MV + ML_V_W
E_MG = E_MO + ML_V_W

DN_HEADS, DN_DK, DN_DV, DN_CHUNK, DN_CONV = 8, 128, 128, 64, 5
DN_QK_W, DN_V_W = DN_HEADS * DN_DK, DN_HEADS * DN_DV
DN_CONV_CH = 2 * DN_QK_W + DN_V_W
ODD_IN = DN_CONV_CH + DN_V_W + 4 * DN_HEADS
ODD_PAD = 4224
O_Z = DN_CONV_CH
O_A = O_Z + DN_V_W

N_EXPERTS, TOP_K, D_EXPERT = 32, 4, 1024
SWIGLU_LIMIT, SWIGLU_ALPHA = 7.0, 1.702
MOE_BM = 256

VMEM_LIMIT = 56 * 1024 * 1024


def _cp(sem):
    return pltpu.CompilerParams(dimension_semantics=sem, vmem_limit_bytes=VMEM_LIMIT)


def _mm(a, b):
    return jnp.dot(a.astype(MXU_DT), b.astype(MXU_DT), preferred_element_type=F32)


def _mm_nt(a, b):
    return lax.dot_general(a.astype(MXU_DT), b.astype(MXU_DT), (((1,), (1,)), ((), ())),
                           preferred_element_type=F32)


def _mm_tn(a, b):
    return lax.dot_general(a.astype(MXU_DT), b.astype(MXU_DT), (((0,), (0,)), ((), ())),
                           preferred_element_type=F32)


def _silu(x):
    return x * jax.nn.sigmoid(x)


def _adaln_kernel(c_ref, w_ref, b_ref, o_ref):
    o_ref[0] = _mm(_silu(c_ref[...]), w_ref[0]) + b_ref[0]


def adaln_all(cond, ada_w, ada_b):
    depth, d, n = ada_w.shape
    tn = 1536
    return pl.pallas_call(
        _adaln_kernel,
        grid=(depth, n // tn),
        in_specs=[pl.BlockSpec(cond.shape, lambda l, j: (0, 0)),
                  pl.BlockSpec((1, d, tn), lambda l, j: (l, 0, j)),
                  pl.BlockSpec((1, 1, tn), lambda l, j: (l, 0, j))],
        out_specs=pl.BlockSpec((1, cond.shape[0], tn), lambda l, j: (l, 0, j)),
        out_shape=jax.ShapeDtypeStruct((depth, cond.shape[0], n), F32),
        compiler_params=_cp(("arbitrary", "arbitrary")),
        name="adaln",
    )(cond, ada_w, ada_b.reshape(depth, 1, n))


def _norm_mod(x, gain, shift, scale):
    y = x * lax.rsqrt(jnp.mean(x * x, axis=-1, keepdims=True) + EPS)
    return (y * gain) * (1.0 + scale) + shift


def _inproj_kernel(x_ref, mod_ref, g_ref, w_ref, o_ref, *, n_chunk):
    h = _norm_mod(x_ref[0], g_ref[...], mod_ref[0:1, :], mod_ref[1:2, :]).astype(MXU_DT)
    for n0 in range(0, o_ref.shape[-1], n_chunk):
        o_ref[0, :, n0:n0 + n_chunk] = jnp.dot(h, w_ref[:, n0:n0 + n_chunk], preferred_element_type=F32)


def in_projection(x, mod, gain, w, n_chunk):
    b, t, d = x.shape
    n = w.shape[1]
    return pl.pallas_call(
        functools.partial(_inproj_kernel, n_chunk=n_chunk),
        grid=(b, t // TM),
        in_specs=[pl.BlockSpec((1, TM, d), lambda i, j: (i, j, 0)),
                  pl.BlockSpec((None, None, 6, d), lambda i, j: (i, jnp.minimum(j, 1), 0, 0)),
                  pl.BlockSpec((1, d), lambda i, j: (0, 0)),
                  pl.BlockSpec((d, n), lambda i, j: (0, 0))],
        out_specs=pl.BlockSpec((1, TM, n), lambda i, j: (i, j, 0)),
        out_shape=jax.ShapeDtypeStruct((b, t, n), F32),
        compiler_params=_cp(("parallel", "arbitrary")),
        name="in_projection",
    )(x, mod, gain.reshape(1, d), w)


def _na_block_start(j):
    return jnp.clip(NA_RB * (j - 1) - NA_WIN_R // 2, 0, GRID_W - NA_KROWS)


def na_bias_tables(rpb):
    rows = GRID_W
    out = []
    for rb in (0, 1, rows // NA_RB - 1):
        start = int(np.clip(NA_RB * rb - NA_WIN_R // 2, 0, rows - NA_KROWS))
        rq = NA_RB * rb + np.arange(NA_QB) // GRID_W
        cq = np.arange(NA_QB) % GRID_W
        rk = start + np.arange(NA_KU) // GRID_W
        ck = np.arange(NA_KU) % GRID_W
        r0 = np.clip(rq - NA_WIN_R // 2, 0, rows - NA_WIN_R)
        c0 = np.clip(cq - NA_WIN_C // 2, 0, GRID_W - NA_WIN_C)
        ok = ((rk[None, :] >= r0[:, None]) & (rk[None, :] < r0[:, None] + NA_WIN_R)
              & (ck[None, :] >= c0[:, None]) & (ck[None, :] < c0[:, None] + NA_WIN_C))
        ir = np.clip(rk[None, :] - rq[:, None] + NA_WIN_R - 1, 0, 2 * NA_WIN_R - 2)
        ic = np.clip(ck[None, :] - cq[:, None] + NA_WIN_C - 1, 0, 2 * NA_WIN_C - 2)
        out.append(jnp.where(ok[None], rpb.astype(F32)[:, ir, ic], NEG))
    out.append(jnp.full_like(out[0], NEG))
    return jnp.stack(out, axis=1)


def _na_kernel(q_ref, k_ref, v_ref, qg_ref, kg_ref, bias_ref, o_ref, kn_ref):
    j = pl.program_id(2)
    is_h0 = lax.broadcasted_iota(jnp.int32, (1, LANES), 1) < NA_DH

    def headnorm(x, gain):
        x2 = x * x
        s0 = jnp.sum(jnp.where(is_h0, x2, 0.0), axis=-1, keepdims=True)
        s1 = jnp.sum(jnp.where(is_h0, 0.0, x2), axis=-1, keepdims=True)
        ms = jnp.where(is_h0, s0, s1) * (1.0 / NA_DH)
        return (x * lax.rsqrt(ms + EPS)) * gain

    @pl.when(j == 0)
    def _():
        kn_ref[...] = headnorm(k_ref[0], kg_ref[...]).astype(MXU_DT)

    qn = headnorm(q_ref[0], qg_ref[...]) * (NA_DH ** -0.5)
    off = pl.multiple_of(CTX_LEN + _na_block_start(j) * GRID_W, GRID_W)
    k_loc = kn_ref[pl.ds(off, NA_KU), :]
    k_ctx = kn_ref[0:CTX_LEN, :]
    v_loc = v_ref[0, pl.ds(off, NA_KU), :].astype(MXU_DT)
    v_ctx = v_ref[0, 0:CTX_LEN, :].astype(MXU_DT)
    outs = []
    for h in range(2):
        qm = jnp.where(is_h0 if h == 0 else jnp.logical_not(is_h0), qn, 0.0)
        s_loc = _mm_nt(qm, k_loc) + bias_ref[h, 0]
        s_ctx = _mm_nt(qm, k_ctx)
        m = jnp.maximum(jnp.max(s_loc, axis=-1, keepdims=True), jnp.max(s_ctx, axis=-1, keepdims=True))
        p_loc = jnp.exp(s_loc - m)
        p_ctx = jnp.exp(s_ctx - m)
        den = jnp.sum(p_loc, axis=-1, keepdims=True) + jnp.sum(p_ctx, axis=-1, keepdims=True)
        outs.append((_mm(p_loc, v_loc) + _mm(p_ctx, v_ctx)) / den)
    o_ref[0] = jnp.where(is_h0, outs[0], outs[1])


def neighbourhood_attention(proj, q_gain, k_gain, bias):
    b, t, _ = proj.shape
    nq = t // NA_QB
    n_rb = GRID_W // NA_RB

    def bias_idx(i, hp, j):
        var = jnp.where(j == 0, 3, jnp.where(j == 1, 0, jnp.where(j == n_rb, 2, 1)))
        return (hp, var, 0, 0)

    tile2 = lambda g: jnp.tile(g.astype(F32), 2).reshape(1, LANES)
    return pl.pallas_call(
        _na_kernel,
        grid=(b, NA_HEADS // 2, nq),
        in_specs=[pl.BlockSpec((1, NA_QB, LANES), lambda i, hp, j: (i, j, E_Q // LANES + hp)),
                  pl.BlockSpec((1, t, LANES), lambda i, hp, j: (i, 0, E_K // LANES + hp)),
                  pl.BlockSpec((1, t, LANES), lambda i, hp, j: (i, 0, E_V // LANES + hp)),
                  pl.BlockSpec((1, LANES), lambda i, hp, j: (0, 0)),
                  pl.BlockSpec((1, LANES), lambda i, hp, j: (0, 0)),
                  pl.BlockSpec((2, 1, NA_QB, NA_KU), bias_idx)],
        out_specs=pl.BlockSpec((1, NA_QB, LANES), lambda i, hp, j: (i, j, hp)),
        out_shape=jax.ShapeDtypeStruct((b, t, NA_W), F32),
        scratch_shapes=[pltpu.VMEM((t, LANES), MXU_DT)],
        compiler_params=_cp(("parallel", "arbitrary", "arbitrary")),
        name="neighbourhood_attention",
    )(proj, proj, proj, tile2(q_gain), tile2(k_gain), bias)


def _scan_tile(i, n_tiles, reverse):
    if not reverse:
        return i
    return jnp.where(i == 0, 0, n_tiles - i)


def _tri_masks(n, reverse):
    r = lax.broadcasted_iota(jnp.int32, (n, n), 0)
    c = lax.broadcasted_iota(jnp.int32, (n, n), 1)
    if reverse:
        return c >= r, c > r, r >= c
    return c <= r, c < r, r <= c


def _cumsum_both(x_col, x_row, incl, incl_t):
    cum_col = jnp.sum(jnp.where(incl, x_row, 0.0), axis=1, keepdims=True)
    cum_row = jnp.sum(jnp.where(incl_t, x_col, 0.0), axis=0, keepdims=True)
    return cum_col, cum_row


def _rope(x, cos, sin):
    w = x.shape[-1]
    lane = lax.broadcasted_iota(jnp.int32, (1, w), 1)
    first = (lane % ML_DQK) < ML_DQK // 2
    swapped = jnp.where(first, pltpu.roll(x, w - ML_DQK // 2, 1), pltpu.roll(x, ML_DQK // 2, 1))
    return x * cos + swapped * sin


def _mlstm_kernel(q_ref, k_ref, v_ref, g_ref, gt_ref, gbr_ref, gbc_ref, cos_ref, sin_ref, o_ref,
                  c_ref, n_ref, m_ref, *, reverse):
    i = pl.program_id(1)
    d = 1 if reverse else 0

    @pl.when(i == 0)
    def _():
        c_ref[...] = jnp.zeros_like(c_ref)
        n_ref[...] = jnp.zeros_like(n_ref)
        m_ref[...] = jnp.zeros_like(m_ref)

    q = _rope(q_ref[0], cos_ref[...], sin_ref[...])
    k = _rope(k_ref[0] * (ML_DQK ** -0.5), cos_ref[...], sin_ref[...])
    g_col = g_ref[0] + gbr_ref[...]
    g_row = gt_ref[0] + gbc_ref[...]
    incl, _, incl_t = _tri_masks(ML_CHUNK, reverse)
    is_h0 = lax.broadcasted_iota(jnp.int32, (1, LANES), 1) < ML_DQK
    row_h0 = lax.broadcasted_iota(jnp.int32, (LANES, 1), 0) < ML_DQK
    n_chunks = TM // ML_CHUNK
    order = range(n_chunks - 1, -1, -1) if reverse else range(n_chunks)
    for ci in order:
        tok = slice(ci * ML_CHUNK, (ci + 1) * ML_CHUNK)
        for p in range(ML_HEADS // 2):
            lanes = slice(p * LANES, (p + 1) * LANES)
            qp, kp = q[tok, lanes], k[tok, lanes]
            c_pair = c_ref[p]
            n_pair = n_ref[:, lanes]
            upd_c, upd_n, decs = [], [], []
            for s in range(2):
                h = 2 * p + s
                hm = is_h0 if s == 0 else jnp.logical_not(is_h0)
                li_c = g_col[tok, 8 * d + h:8 * d + h + 1]
                lf_c = jax.nn.log_sigmoid(g_col[tok, 8 * d + 4 + h:8 * d + 5 + h])
                li_r = g_row[8 * d + h:8 * d + h + 1, tok]
                lf_r = jax.nn.log_sigmoid(g_row[8 * d + 4 + h:8 * d + 5 + h, tok])
                m_prev = m_ref[h][:, 0:1]
                cum_c, cum_r = _cumsum_both(lf_c, lf_r, incl, incl_t)
                log_d = jnp.where(incl, cum_c - cum_r + li_r, NEG)
                m_inter = cum_c + m_prev
                m_t = jnp.maximum(jnp.max(log_d, axis=-1, keepdims=True), m_inter)
                qm = jnp.where(hm, qp, 0.0)
                km = jnp.where(hm, kp, 0.0)
                vh = v_ref[0, tok, h * ML_DV:(h + 1) * ML_DV]
                sg = _mm_nt(qm, kp) * jnp.exp(log_d - m_t)
                w_inter = jnp.exp(m_inter - m_t)
                num = _mm(sg, vh) + w_inter * _mm(qm, c_pair)
                den = (jnp.sum(sg, axis=-1, keepdims=True)
                       + w_inter * jnp.sum(qm * n_pair, axis=-1, keepdims=True))
                o_ref[0, 0, tok, h * ML_DV:(h + 1) * ML_DV] = num / jnp.maximum(jnp.abs(den), jnp.exp(-m_t))
                cum_last = jnp.sum(lf_r, axis=-1, keepdims=True)
                log_w = cum_last - cum_c + li_c
                m_new = jnp.maximum(cum_last + m_prev, jnp.max(log_w, axis=0, keepdims=True))
                kw = km * jnp.exp(log_w - m_new)
                upd_c.append(_mm_tn(kw, vh))
                upd_n.append(jnp.sum(kw, axis=0, keepdims=True))
                decs.append(jnp.exp(cum_last + m_prev - m_new))
                m_ref[h] = jnp.broadcast_to(m_new, (1, LANES))
            c_ref[p] = jnp.where(row_h0, decs[0], decs[1]) * c_pair + upd_c[0] + upd_c[1]
            n_ref[:, lanes] = jnp.where(is_h0, decs[0], decs[1]) * n_pair + upd_n[0] + upd_n[1]


def mlstm_direction(proj, gates_t, gate_b, cos, sin, reverse):
    b, t, _ = proj.shape
    n_tiles = t // TM
    tile = functools.partial(_scan_tile, n_tiles=n_tiles, reverse=reverse)
    gb = gate_b.astype(F32).reshape(16)
    gb_row = jnp.zeros((1, LANES), F32).at[0, :16].set(gb)
    gb_col = gb.reshape(16, 1)
    return pl.pallas_call(
        functools.partial(_mlstm_kernel, reverse=reverse),
        grid=(b, n_tiles),
        in_specs=[pl.BlockSpec((1, TM, ML_QK_W), lambda bi, i: (bi, tile(i), E_MQ // ML_QK_W)),
                  pl.BlockSpec((1, TM, ML_QK_W), lambda bi, i: (bi, tile(i), E_MK // ML_QK_W)),
                  pl.BlockSpec((1, TM, ML_V_W), lambda bi, i: (bi, tile(i), E_MV // ML_V_W)),
                  pl.BlockSpec((1, TM, LANES), lambda bi, i: (bi, tile(i), E_MG // LANES)),
                  pl.BlockSpec((1, 16, TM), lambda bi, i: (bi, 0, tile(i))),
                  pl.BlockSpec((1, LANES), lambda bi, i: (0, 0)),
                  pl.BlockSpec((16, 1), lambda bi, i: (0, 0)),
                  pl.BlockSpec((TM, ML_QK_W), lambda bi, i: (tile(i), 0)),
                  pl.BlockSpec((TM, ML_QK_W), lambda bi, i: (tile(i), 0))],
        out_specs=pl.BlockSpec((1, 1, TM, ML_V_W), lambda bi, i: (0, bi, tile(i), 0)),
        out_shape=jax.ShapeDtypeStruct((1, b, t, ML_V_W), F32),
        scratch_shapes=[pltpu.VMEM((ML_HEADS // 2, LANES, ML_DV), F32),
                        pltpu.VMEM((1, ML_QK_W), F32),
                        pltpu.VMEM((ML_HEADS, 1, LANES), F32)],
        compiler_params=_cp(("parallel", "arbitrary")),
        name="mlstm_bwd" if reverse else "mlstm_fwd",
    )(proj, proj, proj, proj, gates_t, gb_row, gb_col, cos, sin)[0]


def rope_tables(t):
    s = t - CTX_LEN
    tok = jnp.arange(s)
    n_freq = ML_DQK // 4
    inv = ROPE_THETA ** (-jnp.arange(n_freq, dtype=F32) / n_freq)
    ang = jnp.concatenate([(tok // GRID_W).astype(F32)[:, None] * inv,
                           (tok % GRID_W).astype(F32)[:, None] * inv], axis=-1)
    cos, sin = jnp.cos(ang), jnp.sin(ang)
    cos_h = jnp.concatenate([cos, cos], axis=-1)
    sin_h = jnp.concatenate([-sin, sin], axis=-1)
    cos_f = jnp.concatenate([jnp.ones((CTX_LEN, ML_DQK), F32), cos_h], axis=0)
    sin_f = jnp.concatenate([jnp.zeros((CTX_LEN, ML_DQK), F32), sin_h], axis=0)
    return jnp.tile(cos_f, (1, ML_HEADS)), jnp.tile(sin_f, (1, ML_HEADS))


def _dnconv_kernel(x_ref, w_ref, o_ref):
    cb = pl.program_id(1)
    x = x_ref[0]
    t = x.shape[0]
    tok = lax.broadcasted_iota(jnp.int32, (t, 1), 0)
    seg_lo = jnp.where(tok < CTX_LEN, 0, CTX_LEN)
    seg_hi = jnp.where(tok < CTX_LEN, CTX_LEN, t)
    acc = jnp.zeros_like(x)
    for j in range(DN_CONV):
        o = j - DN_CONV // 2
        xs = x if o == 0 else pltpu.roll(x, (-o) % t, 0)
        ok = (tok + o >= seg_lo) & (tok + o < seg_hi)
        acc = acc + jnp.where(ok, xs, 0.0) * w_ref[j:j + 1, :]
    y = _silu(acc)
    nrm = lax.rsqrt(jnp.sum(y * y, axis=-1, keepdims=True) + EPS)
    q_heads = DN_QK_W // LANES
    o_ref[0] = jnp.where(cb < q_heads, (y * nrm) * (DN_DK ** -0.5), jnp.where(cb < 2 * q_heads, y * nrm, y))


def dn_conv(proj, conv_w):
    b, t, _ = proj.shape
    return pl.pallas_call(
        _dnconv_kernel,
        grid=(b, DN_CONV_CH // LANES),
        in_specs=[pl.BlockSpec((1, t, LANES), lambda i, c: (i, 0, c)),
                  pl.BlockSpec((DN_CONV, LANES), lambda i, c: (0, c))],
        out_specs=pl.BlockSpec((1, t, LANES), lambda i, c: (i, 0, c)),
        out_shape=jax.ShapeDtypeStruct((b, t, DN_CONV_CH), F32),
        compiler_params=_cp(("parallel", "arbitrary")),
        name="dn_conv",
    )(proj, conv_w.astype(F32).T)


def _gdn_kernel(q_ref, k_ref, v_ref, ab_ref, at_ref, par_ref, parc_ref, o_ref, s_ref, *, reverse):
    h = pl.program_id(1)
    i = pl.program_id(2)
    d = 1 if reverse else 0

    @pl.when(i == 0)
    def _():
        s_ref[...] = jnp.zeros_like(s_ref)

    incl, strict, incl_t = _tri_masks(DN_CHUNK, reverse)
    r = lax.broadcasted_iota(jnp.int32, (DN_CHUNK, DN_CHUNK), 0)
    c = lax.broadcasted_iota(jnp.int32, (DN_CHUNK, DN_CHUNK), 1)
    eye = (r == c).astype(F32)
    lane = lax.broadcasted_iota(jnp.int32, (1, LANES), 1)
    sub = lax.broadcasted_iota(jnp.int32, (16, 1), 0)
    col = 8 * d + h
    ab = ab_ref[0]
    neg_a = -jnp.exp(par_ref[0:1, :])
    g_all = neg_a * jax.nn.softplus(ab + par_ref[1:2, :])
    g_col_all = jnp.sum(jnp.where(lane == col, g_all, 0.0), axis=-1, keepdims=True)
    beta_all = jnp.sum(jnp.where(lane == 16 + col, jax.nn.sigmoid(ab), 0.0), axis=-1, keepdims=True)
    gt_all = -jnp.exp(parc_ref[:, 0:1]) * jax.nn.softplus(at_ref[0] + parc_ref[:, 1:2])
    g_row_all = jnp.sum(jnp.where(sub == col, gt_all, 0.0), axis=0, keepdims=True)

    n_chunks = TM // DN_CHUNK
    order = range(n_chunks - 1, -1, -1) if reverse else range(n_chunks)
    pre = {}
    for ci in order:
        tok = slice(ci * DN_CHUNK, (ci + 1) * DN_CHUNK)
        q, k, v = q_ref[0, tok, :], k_ref[0, tok, :], v_ref[0, tok, :]
        g_c, beta = g_col_all[tok], beta_all[tok]
        gc_c, gc_r = _cumsum_both(g_c, g_row_all[:, tok], incl, incl_t)
        decay = jnp.exp(jnp.where(incl, gc_c - gc_r, NEG))
        kb = k * beta
        a_mat = jnp.where(strict, _mm_nt(kb, k) * decay, 0.0)
        pw = -a_mat
        t_inv = eye + pw
        for _ in range(5):
            pw = _mm(pw, pw)
            t_inv = t_inv + _mm(t_inv, pw)
        u = _mm(t_inv, v * beta)
        w = _mm(t_inv, kb * jnp.exp(gc_c))
        attn = _mm_nt(q, k) * decay
        g_last = jnp.sum(g_c, axis=0, keepdims=True)
        pre[ci] = (q * jnp.exp(gc_c), k * jnp.exp(g_last - gc_c), u, w, attn, jnp.exp(g_last))
    s_mat = s_ref[...]
    for ci in order:
        tok = slice(ci * DN_CHUNK, (ci + 1) * DN_CHUNK)
        qg, kg, u, w, attn, dec = pre[ci]
        v_new = u - _mm(w, s_mat)
        o_ref[0, 0, tok, :] = _mm(qg, s_mat) + _mm(attn, v_new)
        s_mat = s_mat * dec + _mm_tn(kg, v_new)
    s_ref[...] = s_mat


def gdn_direction(qkv, proj, gates_t, a_log, dt_bias, reverse):
    b, t, _ = qkv.shape
    n_tiles = t // TM
    tile = functools.partial(_scan_tile, n_tiles=n_tiles, reverse=reverse)
    par = jnp.zeros((2, LANES), F32)
    par = par.at[0, :16].set(a_log.astype(F32).reshape(16)).at[1, :16].set(dt_bias.astype(F32).reshape(16))
    parc = par[:, :16].T
    nh = DN_HEADS
    return pl.pallas_call(
        functools.partial(_gdn_kernel, reverse=reverse),
        grid=(b, nh, n_tiles),
        in_specs=[pl.BlockSpec((1, TM, LANES), lambda bi, h, i: (bi, tile(i), h)),
                  pl.BlockSpec((1, TM, LANES), lambda bi, h, i: (bi, tile(i), nh + h)),
                  pl.BlockSpec((1, TM, LANES), lambda bi, h, i: (bi, tile(i), 2 * nh + h)),
                  pl.BlockSpec((1, TM, LANES), lambda bi, h, i: (bi, tile(i), O_A // LANES)),
                  pl.BlockSpec((1, 16, TM), lambda bi, h, i: (bi, 0, tile(i))),
                  pl.BlockSpec((2, LANES), lambda bi, h, i: (0, 0)),
                  pl.BlockSpec((16, 2), lambda bi, h, i: (0, 0))],
        out_specs=pl.BlockSpec((1, 1, TM, LANES), lambda bi, h, i: (0, bi, tile(i), h)),
        out_shape=jax.ShapeDtypeStruct((1, b, t, DN_V_W), F32),
        scratch_shapes=[pltpu.VMEM((DN_DK, DN_DV), F32)],
        compiler_params=_cp(("parallel", "arbitrary", "arbitrary")),
        name="gdn_bwd" if reverse else "gdn_fwd",
    )(qkv, qkv, qkv, proj, gates_t, par, parc)[0]


def _headwise_rms(x, gain, width):
    parts = []
    for h0 in range(0, x.shape[-1], width):
        xh = x[:, h0:h0 + width]
        parts.append((xh * lax.rsqrt(jnp.mean(xh * xh, axis=-1, keepdims=True) + EPS)) * gain)
    return parts


def _outproj_kernel(*refs, even):
    if even:
        (a_ref, hf_ref, hb_ref, og_ref, gain_ref, x_ref, mod_ref, g2_ref, w_ref, rw_ref, rb_ref,
         xo_ref, hm_ref, tv_ref, ti_ref) = refs
        hs = _headwise_rms(hf_ref[0] + hb_ref[0], gain_ref[...], ML_DV)
        og = og_ref[0]
        mix = [a_ref[0]] + [hs[h] * jax.nn.sigmoid(og[:, h * ML_DV:(h + 1) * ML_DV]) for h in range(ML_HEADS)]
    else:
        (of_ref, ob_ref, z_ref, gain_ref, x_ref, mod_ref, g2_ref, w_ref, rw_ref, rb_ref,
         xo_ref, hm_ref, tv_ref, ti_ref) = refs
        hs = _headwise_rms(of_ref[0] + ob_ref[0], gain_ref[...], DN_DV)
        z = z_ref[0]
        mix = [hs[h] * _silu(z[:, h * DN_DV:(h + 1) * DN_DV]) for h in range(DN_HEADS)]
    mix = jnp.concatenate(mix, axis=-1)
    x_new = x_ref[0] + mod_ref[2:3, :] * _mm(mix, w_ref[...])
    xo_ref[0] = x_new
    hm = _norm_mod(x_new, g2_ref[...], mod_ref[3:4, :], mod_ref[4:5, :])
    hm_ref[0] = hm.astype(hm_ref.dtype)
    logits = jnp.dot(hm, rw_ref[...], preferred_element_type=F32, precision=lax.Precision.HIGHEST) + rb_ref[...]
    lane = lax.broadcasted_iota(jnp.int32, logits.shape, 1)
    vals = jnp.zeros_like(logits)
    idxs = jnp.zeros(logits.shape, jnp.int32)
    work = jnp.where(lane < N_EXPERTS, logits, -jnp.inf)
    for kk in range(TOP_K):
        mx = jnp.max(work, axis=-1, keepdims=True)
        am = jnp.min(jnp.where(work == mx, lane, LANES), axis=-1, keepdims=True)
        vals = jnp.where(lane == kk, mx, vals)
        idxs = jnp.where(lane == kk, am, idxs)
        work = jnp.where(lane == am, -jnp.inf, work)
    ex = jnp.where(lane < TOP_K, jnp.exp(vals - jnp.max(jnp.where(lane < TOP_K, vals, -jnp.inf), axis=-1, keepdims=True)), 0.0)
    tv_ref[0] = ex / jnp.sum(ex, axis=-1, keepdims=True)
    ti_ref[0] = idxs


def out_projection(even, mixer_in, gain, x, mod, g2, w_out, router_w, router_b):
    b, t, d = x.shape
    tok = lambda wdt, blk: pl.BlockSpec((1, TM, wdt), lambda i, j: (i, j, blk))
    if even:
        a, hf, hb, proj = mixer_in
        ins = [a, hf, hb, proj]
        specs = [tok(NA_W, 0), tok(ML_V_W, 0), tok(ML_V_W, 0), tok(ML_V_W, E_MO // ML_V_W)]
    else:
        of, ob, proj = mixer_in
        ins = [of, ob, proj]
        specs = [tok(DN_V_W, 0), tok(DN_V_W, 0), tok(DN_V_W, O_Z // DN_V_W)]
    rw = jnp.zeros((d, LANES), F32).at[:, :N_EXPERTS].set(router_w.astype(F32))
    rb = jnp.zeros((1, LANES), F32).at[0, :N_EXPERTS].set(router_b.astype(F32))
    const = lambda shape: pl.BlockSpec(shape, lambda i, j: (0,) * len(shape))
    ins += [gain.astype(F32).reshape(1, LANES), x, mod, g2.reshape(1, d), w_out.astype(MXU_DT), rw, rb]
    specs += [const((1, LANES)), tok(d, 0),
              pl.BlockSpec((None, None, 6, d), lambda i, j: (i, jnp.minimum(j, 1), 0, 0)),
              const((1, d)), const(w_out.shape), const((d, LANES)), const((1, LANES))]
    return pl.pallas_call(
        functools.partial(_outproj_kernel, even=even),
        grid=(b, t // TM),
        in_specs=specs,
        out_specs=[tok(d, 0), tok(d, 0), tok(LANES, 0), tok(LANES, 0)],
        out_shape=[jax.ShapeDtypeStruct((b, t, d), F32), jax.ShapeDtypeStruct((b, t, d), MXU_DT),
                   jax.ShapeDtypeStruct((b, t, LANES), F32), jax.ShapeDtypeStruct((b, t, LANES), jnp.int32)],
        compiler_params=_cp(("parallel", "arbitrary")),
        name="out_projection_even" if even else "out_projection_odd",
    )(*ins)


def _experts_kernel(be_ref, nb_ref, x_ref, w1g_ref, w1l_ref, b1g_ref, b1l_ref, w2_ref, b2_ref, o_ref):
    @pl.when(pl.program_id(0) < nb_ref[0])
    def _():
        x = x_ref[...]
        glu = jnp.minimum(jnp.dot(x, w1g_ref[0], preferred_element_type=F32) + b1g_ref[0], SWIGLU_LIMIT)
        lin = jnp.clip(jnp.dot(x, w1l_ref[0], preferred_element_type=F32) + b1l_ref[0], -SWIGLU_LIMIT, SWIGLU_LIMIT)
        act = glu * jax.nn.sigmoid(SWIGLU_ALPHA * glu) * (lin + 1.0)
        o_ref[...] = _mm(act, w2_ref[0]) + b2_ref[0]

    @pl.when(pl.program_id(0) >= nb_ref[0])
    def _():
        o_ref[...] = jnp.zeros_like(o_ref)


def experts_ffn(xs, block_exp, n_used, w1g, w1l, b1g, b1l, w2, b2):
    n_slots, d = xs.shape
    n_blocks = n_slots // MOE_BM
    e, _, de = w1g.shape
    wspec = lambda shape: pl.BlockSpec((1,) + shape, lambda i, be, nb: (be[i], 0, 0))
    return pl.pallas_call(
        _experts_kernel,
        grid_spec=pltpu.PrefetchScalarGridSpec(
            num_scalar_prefetch=2,
            grid=(n_blocks,),
            in_specs=[pl.BlockSpec((MOE_BM, d), lambda i, be, nb: (jnp.minimum(i, nb[0] - 1), 0)),
                      wspec((d, de)), wspec((d, de)), wspec((1, de)), wspec((1, de)),
                      wspec((de, d)), wspec((1, d))],
            out_specs=pl.BlockSpec((MOE_BM, d), lambda i, be, nb: (i, 0)),
        ),
        out_shape=jax.ShapeDtypeStruct((n_slots, d), F32),
        compiler_params=_cp(("arbitrary",)),
        name="experts_ffn",
    )(block_exp, n_used, xs, w1g, w1l, b1g, b1l, w2, b2)


def _combine_kernel(x_ref, y_ref, gate_ref, mod_ref, o_ref):
    gate = gate_ref[0]
    f = y_ref[0, 0] * gate[:, 0:1]
    for kk in range(1, TOP_K):
        f = f + y_ref[0, kk] * gate[:, kk:kk + 1]
    o_ref[0] = x_ref[0] + mod_ref[5:6, :] * f


def moe_combine(x, yg, gate, mod):
    b, t, d = x.shape
    return pl.pallas_call(
        _combine_kernel,
        grid=(b, t // TM),
        in_specs=[pl.BlockSpec((1, TM, d), lambda i, j: (i, j, 0)),
                  pl.BlockSpec((1, TOP_K, TM, d), lambda i, j: (i, 0, j, 0)),
                  pl.BlockSpec((1, TM, LANES), lambda i, j: (i, j, 0)),
                  pl.BlockSpec((None, None, 6, d), lambda i, j: (i, jnp.minimum(j, 1), 0, 0))],
        out_specs=pl.BlockSpec((1, TM, d), lambda i, j: (i, j, 0)),
        out_shape=jax.ShapeDtypeStruct((b, t, d), F32),
        compiler_params=_cp(("parallel", "arbitrary")),
        name="moe_combine",
    )(x, yg, gate, mod)


def moe_layer(x_new, hm, gate, top_idx, mod, w1, b1, w2, b2):
    b, t, d = x_new.shape
    n_tok = b * t
    n_asg = n_tok * TOP_K
    flat_exp = top_idx[:, :, :TOP_K].reshape(n_asg)
    onehot = (flat_exp[:, None] == jnp.arange(N_EXPERTS, dtype=jnp.int32)[None, :]).astype(jnp.int32)
    csum = jnp.cumsum(onehot, axis=0)
    counts = csum[-1]
    rank = jnp.sum((csum - onehot) * onehot, axis=1)
    padded = (counts + MOE_BM - 1) // MOE_BM * MOE_BM
    pad_end = jnp.cumsum(padded)
    dest = (pad_end - padded)[flat_exp] + rank
    n_blocks = -(-n_asg // MOE_BM) + N_EXPERTS
    n_slots = n_blocks * MOE_BM
    slot_tok = jnp.zeros((n_slots,), jnp.int32).at[dest].set(jnp.arange(n_asg, dtype=jnp.int32) // TOP_K)
    block_exp = jnp.minimum(jnp.searchsorted(pad_end, jnp.arange(n_blocks, dtype=jnp.int32) * MOE_BM, side='right'),
                            N_EXPERTS - 1).astype(jnp.int32)
    n_used = (pad_end[-1] // MOE_BM).astype(jnp.int32).reshape(1)
    xs = hm.reshape(n_tok, d)[slot_tok]
    w1g = w1[:, :, 0::2].astype(MXU_DT)
    w1l = w1[:, :, 1::2].astype(MXU_DT)
    b1g = b1[:, None, 0::2].astype(F32)
    b1l = b1[:, None, 1::2].astype(F32)
    y = experts_ffn(xs, block_exp, n_used, w1g, w1l, b1g, b1l, w2.astype(MXU_DT), b2[:, None, :].astype(F32))
    yg = y[dest.reshape(b, t, TOP_K).transpose(0, 2, 1)]
    return moe_combine(x_new, yg, gate, mod)


def _pad_cols(w, n):
    return jnp.pad(w, ((0, 0), (0, n - w.shape[1]))).astype(MXU_DT)


def kernel(x, c, ctx, c_ctx, ada_w, ada_b, norm1_g, norm2_g, ev_w_in, ev_w_out, na_q_gain, na_k_gain, na_rpb,
           ml_gate_b, ml_out_gain, od_w_in, od_w_out, dn_conv_w, dn_a_log, dn_dt_bias, dn_out_gain, router_w,
           router_b, exp_w1, exp_b1, exp_w2, exp_b2):
    b, s, d = x.shape
    depth = ada_w.shape[0]
    t = CTX_LEN + s
    xs = jnp.concatenate([ctx, x], axis=1)
    cond = jnp.zeros((16, d), F32).at[:b].set(c).at[b].set(c_ctx)
    mods = adaln_all(cond, ada_w, ada_b)
    mod_lat = mods[:, :b].reshape(depth, b, 1, 6, d)
    mod_ctx = jnp.broadcast_to(mods[:, b].reshape(depth, 1, 1, 6, d), (depth, b, 1, 6, d))
    mod_all = jnp.concatenate([mod_ctx, mod_lat], axis=2)
    cos, sin = rope_tables(t)
    for layer in range(depth):
        mod = mod_all[layer]
        if layer % 2 == 0:
            e = layer // 2
            proj = in_projection(xs, mod, norm1_g[layer], _pad_cols(ev_w_in[e], EVEN_PAD), 640)
            a = neighbourhood_attention(proj, na_q_gain[e], na_k_gain[e], na_bias_tables(na_rpb[e]))
            gates_t = proj[:, :, E_MG:E_MG + 16].transpose(0, 2, 1)
            hf = mlstm_direction(proj, gates_t, ml_gate_b[e], cos, sin, False)
            hb = mlstm_direction(proj, gates_t, ml_gate_b[e], cos, sin, True)
            mixer_in, gain, w_out = (a, hf, hb, proj), ml_out_gain[e], ev_w_out[e]
        else:
            o = layer // 2
            proj = in_projection(xs, mod, norm1_g[layer], _pad_cols(od_w_in[o], ODD_PAD), 1056)
            qkv = dn_conv(proj, dn_conv_w[o])
            gates_t = proj[:, :, O_A:O_A + 16].transpose(0, 2, 1)
            of = gdn_direction(qkv, proj, gates_t, dn_a_log[o], dn_dt_bias[o], False)
            ob = gdn_direction(qkv, proj, gates_t, dn_a_log[o], dn_dt_bias[o], True)
            mixer_in, gain, w_out = (of, ob, proj), dn_out_gain[o], od_w_out[o]
        x_new, hm, gate, top_idx = out_projection(layer % 2 == 0, mixer_in, gain, xs, mod, norm2_g[layer], w_out,
                                                  router_w[layer], router_b[layer])
        xs = moe_layer(x_new, hm, gate, top_idx, mod, exp_w1[layer], exp_b1[layer], exp_w2[layer], exp_b2[layer])
    return xs[:, CTX_LEN:]
```

```python
import functools

import numpy as np
import jax
import jax.numpy as jnp
from jax import lax
from jax.experimental import pallas as pl
from jax.experimental.pallas import tpu as pltpu

F32 = jnp.float32
MXU_DT = jnp.bfloat16
EPS = 1e-6
NEG = -1e30

D_MODEL = 1024
GRID_W = 64
CTX_LEN = 256
TM = 256
LANES = 128

NA_HEADS, NA_DH = 8, 64
NA_W = NA_HEADS * NA_DH
NA_WIN_R, NA_WIN_C = 8, 16
NA_RB = 4
NA_QB = NA_RB * GRID_W
NA_KROWS = NA_RB + NA_WIN_R - 1
NA_KU = NA_KROWS * GRID_W

ML_HEADS, ML_DQK, ML_DV, ML_CHUNK = 4, 64, 128, 64
ML_QK_W, ML_V_W = ML_HEADS * ML_DQK, ML_HEADS * ML_DV
ROPE_THETA = 10000.0
EVEN_IN = 3 * NA_W + 2 * ML_QK_W + 2 * ML_V_W + 4 * ML_HEADS
EVEN_PAD = 3200
E_Q, E_K, E_V = 0, NA_W, 2 * NA_W
E_MQ = 3 * NA_W
E_MK = E_MQ + ML_QK_W
E_MV = E_MK + ML_QK_W
E_MO = E_MV + ML_V_W
E_MG = E_MO + ML_V_W

DN_HEADS, DN_DK, DN_DV, DN_CHUNK, DN_CONV = 8, 128, 128, 64, 5
DN_QK_W, DN_V_W = DN_HEADS * DN_DK, DN_HEADS * DN_DV
DN_CONV_CH = 2 * DN_QK_W + DN_V_W
ODD_IN = DN_CONV_CH + DN_V_W + 4 * DN_HEADS
ODD_PAD = 4224
O_Z = DN_CONV_CH
O_A = O_Z + DN_V_W

N_EXPERTS, TOP_K, D_EXPERT = 32, 4, 1024
SWIGLU_LIMIT, SWIGLU_ALPHA = 7.0, 1.702
MOE_BM = 512

VMEM_LIMIT = 56 * 1024 * 1024


def _cp(sem):
    return pltpu.CompilerParams(dimension_semantics=sem, vmem_limit_bytes=VMEM_LIMIT)


def _mm(a, b):
    return jnp.dot(a.astype(MXU_DT), b.astype(MXU_DT), preferred_element_type=F32)


def _mm_nt(a, b):
    return lax.dot_general(a.astype(MXU_DT), b.astype(MXU_DT), (((1,), (1,)), ((), ())),
                           preferred_element_type=F32)


def _mm_tn(a, b):
    return lax.dot_general(a.astype(MXU_DT), b.astype(MXU_DT), (((0,), (0,)), ((), ())),
                           preferred_element_type=F32)


def _bmm(a, b):
    return jnp.einsum('bij,bjk->bik', a.astype(MXU_DT), b.astype(MXU_DT), preferred_element_type=F32)


def _bmm_nt(a, b):
    return jnp.einsum('bik,bjk->bij', a.astype(MXU_DT), b.astype(MXU_DT), preferred_element_type=F32)


def _bmm_tn(a, b):
    return jnp.einsum('bki,bkj->bij', a.astype(MXU_DT), b.astype(MXU_DT), preferred_element_type=F32)


def _silu(x):
    return x * jax.nn.sigmoid(x)


def _adaln_kernel(c_ref, w_ref, b_ref, o_ref):
    o_ref[0] = _mm(_silu(c_ref[...]), w_ref[0]) + b_ref[0]


def adaln_all(cond, ada_w, ada_b):
    depth, d, n = ada_w.shape
    tn = 1536
    return pl.pallas_call(
        _adaln_kernel,
        grid=(depth, n // tn),
        in_specs=[pl.BlockSpec(cond.shape, lambda l, j: (0, 0)),
                  pl.BlockSpec((1, d, tn), lambda l, j: (l, 0, j)),
                  pl.BlockSpec((1, 1, tn), lambda l, j: (l, 0, j))],
        out_specs=pl.BlockSpec((1, cond.shape[0], tn), lambda l, j: (l, 0, j)),
        out_shape=jax.ShapeDtypeStruct((depth, cond.shape[0], n), F32),
        compiler_params=_cp(("arbitrary", "arbitrary")),
        name="adaln",
    )(cond, ada_w, ada_b.reshape(depth, 1, n))


def _norm_mod(x, gain, shift, scale):
    y = x * lax.rsqrt(jnp.mean(x * x, axis=-1, keepdims=True) + EPS)
    return (y * gain) * (1.0 + scale) + shift


def _inproj_kernel(x_ref, mod_ref, g_ref, w_ref, o_ref, *, n_chunk):
    h = _norm_mod(x_ref[0], g_ref[...], mod_ref[0:1, :], mod_ref[1:2, :]).astype(MXU_DT)
    for n0 in range(0, o_ref.shape[-1], n_chunk):
        o_ref[0, :, n0:n0 + n_chunk] = jnp.dot(h, w_ref[:, n0:n0 + n_chunk], preferred_element_type=F32)


def in_projection(x, mod, gain, w, n_chunk):
    b, t, d = x.shape
    n = w.shape[1]
    return pl.pallas_call(
        functools.partial(_inproj_kernel, n_chunk=n_chunk),
        grid=(b, t // TM),
        in_specs=[pl.BlockSpec((1, TM, d), lambda i, j: (i, j, 0)),
                  pl.BlockSpec((None, None, 6, d), lambda i, j: (i, jnp.minimum(j, 1), 0, 0)),
                  pl.BlockSpec((1, d), lambda i, j: (0, 0)),
                  pl.BlockSpec((d, n), lambda i, j: (0, 0))],
        out_specs=pl.BlockSpec((1, TM, n), lambda i, j: (i, j, 0)),
        out_shape=jax.ShapeDtypeStruct((b, t, n), F32),
        compiler_params=_cp(("parallel", "arbitrary")),
        name="in_projection",
    )(x, mod, gain.reshape(1, d), w)


def _na_block_start(j):
    return jnp.clip(NA_RB * (j - 1) - NA_WIN_R // 2, 0, GRID_W - NA_KROWS)


def na_bias_tables(rpb):
    rows = GRID_W
    n_dr = 2 * NA_WIN_R - 1
    cq = np.arange(GRID_W)[:, None]
    ck = np.arange(GRID_W)[None, :]
    c0 = np.clip(cq - NA_WIN_C // 2, 0, GRID_W - NA_WIN_C)
    ok_c = (ck >= c0) & (ck < c0 + NA_WIN_C)
    ic = np.clip(ck - cq + NA_WIN_C - 1, 0, 2 * NA_WIN_C - 2)
    tiles = jnp.where(ok_c[None, None], rpb.astype(F32)[:, :, ic], NEG)
    tiles = jnp.concatenate([tiles, jnp.full_like(tiles[:, :1], NEG)], axis=1)
    idx = np.full((4, NA_RB, NA_KROWS), n_dr, np.int32)
    for var, rb in enumerate((0, 1, rows // NA_RB - 1)):
        start = int(np.clip(NA_RB * rb - NA_WIN_R // 2, 0, rows - NA_KROWS))
        for qi in range(NA_RB):
            rq = NA_RB * rb + qi
            r0 = int(np.clip(rq - NA_WIN_R // 2, 0, rows - NA_WIN_R))
            for kj in range(NA_KROWS):
                rk = start + kj
                if r0 <= rk < r0 + NA_WIN_R:
                    idx[var, qi, kj] = rk - rq + NA_WIN_R - 1
    full = tiles[:, idx]
    return full.transpose(0, 1, 2, 4, 3, 5).reshape(rpb.shape[0], 4, NA_QB, NA_KU)


def _na_kernel(q_ref, k_ref, v_ref, qg_ref, kg_ref, bias_ref, o_ref, kn_ref):
    j = pl.program_id(2)
    is_h0 = lax.broadcasted_iota(jnp.int32, (1, LANES), 1) < NA_DH

    def headnorm(x, gain):
        x2 = x * x
        s0 = jnp.sum(jnp.where(is_h0, x2, 0.0), axis=-1, keepdims=True)
        s1 = jnp.sum(jnp.where(is_h0, 0.0, x2), axis=-1, keepdims=True)
        ms = jnp.where(is_h0, s0, s1) * (1.0 / NA_DH)
        return (x * lax.rsqrt(ms + EPS)) * gain

    @pl.when(j == 0)
    def _():
        kn_ref[...] = headnorm(k_ref[0], kg_ref[...]).astype(MXU_DT)

    qn = headnorm(q_ref[0], qg_ref[...]) * (NA_DH ** -0.5)
    off = pl.multiple_of(CTX_LEN + _na_block_start(j) * GRID_W, GRID_W)
    k_loc = kn_ref[pl.ds(off, NA_KU), :]
    k_ctx = kn_ref[0:CTX_LEN, :]
    v_loc = v_ref[0, pl.ds(off, NA_KU), :].astype(MXU_DT)
    v_ctx = v_ref[0, 0:CTX_LEN, :].astype(MXU_DT)
    outs = []
    for h in range(2):
        qm = jnp.where(is_h0 if h == 0 else jnp.logical_not(is_h0), qn, 0.0)
        s_loc = _mm_nt(qm, k_loc) + bias_ref[h, 0]
        s_ctx = _mm_nt(qm, k_ctx)
        m = jnp.maximum(jnp.max(s_loc, axis=-1, keepdims=True), jnp.max(s_ctx, axis=-1, keepdims=True))
        p_loc = jnp.exp(s_loc - m)
        p_ctx = jnp.exp(s_ctx - m)
        den = jnp.sum(p_loc, axis=-1, keepdims=True) + jnp.sum(p_ctx, axis=-1, keepdims=True)
        outs.append((_mm(p_loc, v_loc) + _mm(p_ctx, v_ctx)) / den)
    o_ref[0] = jnp.where(is_h0, outs[0], outs[1])


def neighbourhood_attention(proj, q_gain, k_gain, bias):
    b, t, _ = proj.shape
    nq = t // NA_QB
    n_rb = GRID_W // NA_RB

    def bias_idx(i, hp, j):
        var = jnp.where(j == 0, 3, jnp.where(j == 1, 0, jnp.where(j == n_rb, 2, 1)))
        return (hp, var, 0, 0)

    tile2 = lambda g: jnp.tile(g.astype(F32), 2).reshape(1, LANES)
    return pl.pallas_call(
        _na_kernel,
        grid=(b, NA_HEADS // 2, nq),
        in_specs=[pl.BlockSpec((1, NA_QB, LANES), lambda i, hp, j: (i, j, E_Q // LANES + hp)),
                  pl.BlockSpec((1, t, LANES), lambda i, hp, j: (i, 0, E_K // LANES + hp)),
                  pl.BlockSpec((1, t, LANES), lambda i, hp, j: (i, 0, E_V // LANES + hp)),
                  pl.BlockSpec((1, LANES), lambda i, hp, j: (0, 0)),
                  pl.BlockSpec((1, LANES), lambda i, hp, j: (0, 0)),
                  pl.BlockSpec((2, 1, NA_QB, NA_KU), bias_idx)],
        out_specs=pl.BlockSpec((1, NA_QB, LANES), lambda i, hp, j: (i, j, hp)),
        out_shape=jax.ShapeDtypeStruct((b, t, NA_W), F32),
        scratch_shapes=[pltpu.VMEM((t, LANES), MXU_DT)],
        compiler_params=_cp(("parallel", "arbitrary", "arbitrary")),
        name="neighbourhood_attention",
    )(proj, proj, proj, tile2(q_gain), tile2(k_gain), bias)


def _scan_tile(i, n_tiles, reverse):
    if not reverse:
        return i
    return jnp.where(i == 0, 0, n_tiles - i)


def _tri_masks(n, reverse):
    r = lax.broadcasted_iota(jnp.int32, (n, n), 0)
    c = lax.broadcasted_iota(jnp.int32, (n, n), 1)
    if reverse:
        return c >= r, c > r, r >= c
    return c <= r, c < r, r <= c


def _cumsum_both(x_col, x_row, incl, incl_t):
    cum_col = jnp.sum(jnp.where(incl, x_row, 0.0), axis=1, keepdims=True)
    cum_row = jnp.sum(jnp.where(incl_t, x_col, 0.0), axis=0, keepdims=True)
    return cum_col, cum_row


def _rope(x, cos, sin):
    w = x.shape[-1]
    lane = lax.broadcasted_iota(jnp.int32, (1, w), 1)
    first = (lane % ML_DQK) < ML_DQK // 2
    swapped = jnp.where(first, pltpu.roll(x, w - ML_DQK // 2, 1), pltpu.roll(x, ML_DQK // 2, 1))
    return x * cos + swapped * sin


def _mlstm_kernel(q_ref, k_ref, v_ref, g_ref, gt_ref, gbr_ref, gbc_ref, cos_ref, sin_ref, o_ref,
                  c_ref, n_ref, m_ref, *, reverse):
    i = pl.program_id(1)
    d = 1 if reverse else 0

    @pl.when(i == 0)
    def _():
        c_ref[...] = jnp.zeros_like(c_ref)
        n_ref[...] = jnp.zeros_like(n_ref)
        m_ref[...] = jnp.zeros_like(m_ref)

    q = _rope(q_ref[0], cos_ref[...], sin_ref[...])
    k = _rope(k_ref[0] * (ML_DQK ** -0.5), cos_ref[...], sin_ref[...])
    g_col = g_ref[0] + gbr_ref[...]
    g_row = gt_ref[0] + gbc_ref[...]
    incl, _, incl_t = _tri_masks(ML_CHUNK, reverse)
    is_h0 = lax.broadcasted_iota(jnp.int32, (1, LANES), 1) < ML_DQK
    nc = TM // ML_CHUNK
    nh = ML_HEADS
    order = range(nc - 1, -1, -1) if reverse else range(nc)
    tokc = lambda ci: slice(ci * ML_CHUNK, (ci + 1) * ML_CHUNK)
    pair = lambda h: slice((h // 2) * LANES, (h // 2 + 1) * LANES)
    hmask = lambda h: is_h0 if h % 2 == 0 else jnp.logical_not(is_h0)
    stack = lambda fn: jnp.stack([fn(h, ci) for h in range(nh) for ci in range(nc)], axis=0)
    by_head = lambda a: a.reshape((nh, nc) + a.shape[1:])
    flat = lambda parts: jnp.stack(parts, axis=1).reshape((nh * nc,) + parts[0].shape[1:])
    qm = stack(lambda h, ci: jnp.where(hmask(h), q[tokc(ci), pair(h)], 0.0))
    km = stack(lambda h, ci: jnp.where(hmask(h), k[tokc(ci), pair(h)], 0.0))
    v = stack(lambda h, ci: v_ref[0, tokc(ci), h * ML_DV:(h + 1) * ML_DV])
    li_c = stack(lambda h, ci: g_col[tokc(ci), 8 * d + h:8 * d + h + 1])
    lf_c = jax.nn.log_sigmoid(stack(lambda h, ci: g_col[tokc(ci), 8 * d + 4 + h:8 * d + 5 + h]))
    li_r = stack(lambda h, ci: g_row[8 * d + h:8 * d + h + 1, tokc(ci)])
    lf_r = jax.nn.log_sigmoid(stack(lambda h, ci: g_row[8 * d + 4 + h:8 * d + 5 + h, tokc(ci)]))
    cum_c = jnp.sum(jnp.where(incl, lf_r, 0.0), axis=-1, keepdims=True)
    cum_r = jnp.sum(jnp.where(incl_t, lf_c, 0.0), axis=-2, keepdims=True)
    log_d = jnp.where(incl, cum_c - cum_r + li_r, NEG)
    m_loc = jnp.max(log_d, axis=-1, keepdims=True)
    cum_last = jnp.sum(lf_r, axis=-1, keepdims=True)
    log_w = cum_last - cum_c + li_c
    m_w = jnp.max(log_w, axis=-2, keepdims=True)
    cl4, mw4 = by_head(cum_last), by_head(m_w)
    m = m_ref[...][:, :, 0:1]
    m_prev, m_new = [None] * nc, [None] * nc
    for ci in order:
        m_prev[ci] = m
        m = jnp.maximum(cl4[:, ci] + m, mw4[:, ci])
        m_new[ci] = m
    m_ref[...] = jnp.broadcast_to(m, m_ref.shape)
    m_prev, m_new = flat(m_prev), flat(m_new)
    m_inter = cum_c + m_prev
    m_t = jnp.maximum(m_loc, m_inter)
    sg = _bmm_nt(qm, km) * jnp.exp(log_d - m_t)
    w_inter = jnp.exp(m_inter - m_t)
    kw = km * jnp.exp(log_w - m_new)
    upd_c, upd_n = by_head(_bmm_tn(kw, v)), by_head(jnp.sum(kw, axis=-2, keepdims=True))
    decay = by_head(jnp.exp(cum_last + m_prev - m_new))
    c_mat, n_vec = c_ref[...], n_ref[...]
    c_prev, n_prev = [None] * nc, [None] * nc
    for ci in order:
        c_prev[ci], n_prev[ci] = c_mat, n_vec
        c_mat = decay[:, ci] * c_mat + upd_c[:, ci]
        n_vec = decay[:, ci] * n_vec + upd_n[:, ci]
    c_ref[...] = c_mat
    n_ref[...] = n_vec
    num = _bmm(sg, v) + w_inter * _bmm(qm, flat(c_prev))
    den = jnp.sum(sg, axis=-1, keepdims=True) + w_inter * jnp.sum(qm * flat(n_prev), axis=-1, keepdims=True)
    h_out = by_head(num / jnp.maximum(jnp.abs(den), jnp.exp(-m_t)))
    for h in range(nh):
        for ci in range(nc):
            o_ref[0, 0, tokc(ci), h * ML_DV:(h + 1) * ML_DV] = h_out[h, ci]


def mlstm_direction(proj, gates_t, gate_b, cos, sin, reverse):
    b, t, _ = proj.shape
    n_tiles = t // TM
    tile = functools.partial(_scan_tile, n_tiles=n_tiles, reverse=reverse)
    gb = gate_b.astype(F32).reshape(16)
    gb_row = jnp.zeros((1, LANES), F32).at[0, :16].set(gb)
    gb_col = gb.reshape(16, 1)
    return pl.pallas_call(
        functools.partial(_mlstm_kernel, reverse=reverse),
        grid=(b, n_tiles),
        in_specs=[pl.BlockSpec((1, TM, ML_QK_W), lambda bi, i: (bi, tile(i), E_MQ // ML_QK_W)),
                  pl.BlockSpec((1, TM, ML_QK_W), lambda bi, i: (bi, tile(i), E_MK // ML_QK_W)),
                  pl.BlockSpec((1, TM, ML_V_W), lambda bi, i: (bi, tile(i), E_MV // ML_V_W)),
                  pl.BlockSpec((1, TM, LANES), lambda bi, i: (bi, tile(i), E_MG // LANES)),
                  pl.BlockSpec((1, 16, TM), lambda bi, i: (bi, 0, tile(i))),
                  pl.BlockSpec((1, LANES), lambda bi, i: (0, 0)),
                  pl.BlockSpec((16, 1), lambda bi, i: (0, 0)),
                  pl.BlockSpec((TM, ML_QK_W), lambda bi, i: (tile(i), 0)),
                  pl.BlockSpec((TM, ML_QK_W), lambda bi, i: (tile(i), 0))],
        out_specs=pl.BlockSpec((1, 1, TM, ML_V_W), lambda bi, i: (0, bi, tile(i), 0)),
        out_shape=jax.ShapeDtypeStruct((1, b, t, ML_V_W), F32),
        scratch_shapes=[pltpu.VMEM((ML_HEADS, LANES, ML_DV), F32),
                        pltpu.VMEM((ML_HEADS, 1, LANES), F32),
                        pltpu.VMEM((ML_HEADS, 1, LANES), F32)],
        compiler_params=_cp(("parallel", "arbitrary")),
        name="mlstm_bwd" if reverse else "mlstm_fwd",
    )(proj, proj, proj, proj, gates_t, gb_row, gb_col, cos, sin)[0]


def rope_tables(t):
    s = t - CTX_LEN
    tok = jnp.arange(s)
    n_freq = ML_DQK // 4
    inv = ROPE_THETA ** (-jnp.arange(n_freq, dtype=F32) / n_freq)
    ang = jnp.concatenate([(tok // GRID_W).astype(F32)[:, None] * inv,
                           (tok % GRID_W).astype(F32)[:, None] * inv], axis=-1)
    cos, sin = jnp.cos(ang), jnp.sin(ang)
    cos_h = jnp.concatenate([cos, cos], axis=-1)
    sin_h = jnp.concatenate([-sin, sin], axis=-1)
    cos_f = jnp.concatenate([jnp.ones((CTX_LEN, ML_DQK), F32), cos_h], axis=0)
    sin_f = jnp.concatenate([jnp.zeros((CTX_LEN, ML_DQK), F32), sin_h], axis=0)
    return jnp.tile(cos_f, (1, ML_HEADS)), jnp.tile(sin_f, (1, ML_HEADS))


def _dnconv_kernel(x_ref, w_ref, o_ref):
    cb = pl.program_id(1)
    x = x_ref[0]
    t = x.shape[0]
    tok = lax.broadcasted_iota(jnp.int32, (t, 1), 0)
    seg_lo = jnp.where(tok < CTX_LEN, 0, CTX_LEN)
    seg_hi = jnp.where(tok < CTX_LEN, CTX_LEN, t)
    acc = jnp.zeros_like(x)
    for j in range(DN_CONV):
        o = j - DN_CONV // 2
        xs = x if o == 0 else pltpu.roll(x, (-o) % t, 0)
        ok = (tok + o >= seg_lo) & (tok + o < seg_hi)
        acc = acc + jnp.where(ok, xs, 0.0) * w_ref[j:j + 1, :]
    y = _silu(acc)
    nrm = lax.rsqrt(jnp.sum(y * y, axis=-1, keepdims=True) + EPS)
    q_heads = DN_QK_W // LANES
    o_ref[0] = jnp.where(cb < q_heads, (y * nrm) * (DN_DK ** -0.5), jnp.where(cb < 2 * q_heads, y * nrm, y))


def dn_conv(proj, conv_w):
    b, t, _ = proj.shape
    return pl.pallas_call(
        _dnconv_kernel,
        grid=(b, DN_CONV_CH // LANES),
        in_specs=[pl.BlockSpec((1, t, LANES), lambda i, c: (i, 0, c)),
                  pl.BlockSpec((DN_CONV, LANES), lambda i, c: (0, c))],
        out_specs=pl.BlockSpec((1, t, LANES), lambda i, c: (i, 0, c)),
        out_shape=jax.ShapeDtypeStruct((b, t, DN_CONV_CH), F32),
        compiler_params=_cp(("parallel", "arbitrary")),
        name="dn_conv",
    )(proj, conv_w.astype(F32).T)


def _gdn_kernel(q_ref, k_ref, v_ref, ab_ref, at_ref, par_ref, parc_ref, o_ref, s_ref, *, reverse):
    i = pl.program_id(1)
    d = 1 if reverse else 0

    @pl.when(i == 0)
    def _():
        s_ref[...] = jnp.zeros_like(s_ref)

    incl, strict, incl_t = _tri_masks(DN_CHUNK, reverse)
    r = lax.broadcasted_iota(jnp.int32, (DN_CHUNK, DN_CHUNK), 0)
    c = lax.broadcasted_iota(jnp.int32, (DN_CHUNK, DN_CHUNK), 1)
    eye = (r == c).astype(F32)
    ab = ab_ref[0]
    g_all = -jnp.exp(par_ref[0:1, :]) * jax.nn.softplus(ab + par_ref[1:2, :])
    beta_full = jax.nn.sigmoid(ab)
    gt_all = -jnp.exp(parc_ref[:, 0:1]) * jax.nn.softplus(at_ref[0] + parc_ref[:, 1:2])

    nc = TM // DN_CHUNK
    nh = DN_HEADS
    tokc = lambda ci: slice(ci * DN_CHUNK, (ci + 1) * DN_CHUNK)
    hl = lambda h: slice(h * LANES, (h + 1) * LANES)
    stack = lambda fn: jnp.stack([fn(h, ci) for h in range(nh) for ci in range(nc)], axis=0)
    q = stack(lambda h, ci: q_ref[0, tokc(ci), hl(h)])
    k = stack(lambda h, ci: k_ref[0, tokc(ci), hl(h)])
    v = stack(lambda h, ci: v_ref[0, tokc(ci), hl(h)])
    g_c = stack(lambda h, ci: g_all[tokc(ci), 8 * d + h:8 * d + h + 1])
    beta = stack(lambda h, ci: beta_full[tokc(ci), 16 + 8 * d + h:17 + 8 * d + h])
    g_r = stack(lambda h, ci: gt_all[8 * d + h:8 * d + h + 1, tokc(ci)])
    gc_c = jnp.sum(jnp.where(incl, g_r, 0.0), axis=-1, keepdims=True)
    gc_r = jnp.sum(jnp.where(incl_t, g_c, 0.0), axis=-2, keepdims=True)
    decay = jnp.exp(jnp.where(incl, gc_c - gc_r, NEG))
    kb = k * beta
    a_mat = jnp.where(strict, _bmm_nt(kb, k) * decay, 0.0)
    pw = -a_mat
    t_inv = eye + pw
    for _ in range(5):
        pw = _bmm(pw, pw)
        t_inv = t_inv + _bmm(t_inv, pw)
    egc = jnp.exp(gc_c)
    uw = _bmm(t_inv, jnp.concatenate([v * beta, kb * egc], axis=-1))
    attn = _bmm_nt(q, k) * decay
    g_last = jnp.sum(g_c, axis=-2, keepdims=True)
    kg = k * jnp.exp(g_last - gc_c)
    ng = _bmm_tn(kg, uw)
    au = _bmm(attn, uw)
    lhs = jnp.concatenate([ng[:, :, DN_DV:], q * egc - au[:, :, DN_DV:]], axis=1)
    by_head = lambda a: a.reshape((nh, nc) + a.shape[1:])
    lhs, n_mat, o2, dec = by_head(lhs), by_head(ng[:, :, :DN_DV]), by_head(au[:, :, :DN_DV]), by_head(jnp.exp(g_last))
    s_mat = s_ref[...]
    for ci in (range(nc - 1, -1, -1) if reverse else range(nc)):
        prod = _bmm(lhs[:, ci], s_mat)
        o = prod[:, DN_DK:] + o2[:, ci]
        for h in range(nh):
            o_ref[0, 0, tokc(ci), hl(h)] = o[h]
        s_mat = s_mat * dec[:, ci] + n_mat[:, ci] - prod[:, :DN_DK]
    s_ref[...] = s_mat


def gdn_direction(qkv, proj, gates_t, a_log, dt_bias, reverse):
    b, t, _ = qkv.shape
    n_tiles = t // TM
    tile = functools.partial(_scan_tile, n_tiles=n_tiles, reverse=reverse)
    par = jnp.zeros((2, LANES), F32)
    par = par.at[0, :16].set(a_log.astype(F32).reshape(16)).at[1, :16].set(dt_bias.astype(F32).reshape(16))
    parc = par[:, :16].T
    return pl.pallas_call(
        functools.partial(_gdn_kernel, reverse=reverse),
        grid=(b, n_tiles),
        in_specs=[pl.BlockSpec((1, TM, DN_QK_W), lambda bi, i: (bi, tile(i), 0)),
                  pl.BlockSpec((1, TM, DN_QK_W), lambda bi, i: (bi, tile(i), 1)),
                  pl.BlockSpec((1, TM, DN_V_W), lambda bi, i: (bi, tile(i), 2)),
                  pl.BlockSpec((1, TM, LANES), lambda bi, i: (bi, tile(i), O_A // LANES)),
                  pl.BlockSpec((1, 16, TM), lambda bi, i: (bi, 0, tile(i))),
                  pl.BlockSpec((2, LANES), lambda bi, i: (0, 0)),
                  pl.BlockSpec((16, 2), lambda bi, i: (0, 0))],
        out_specs=pl.BlockSpec((1, 1, TM, DN_V_W), lambda bi, i: (0, bi, tile(i), 0)),
        out_shape=jax.ShapeDtypeStruct((1, b, t, DN_V_W), F32),
        scratch_shapes=[pltpu.VMEM((DN_HEADS, DN_DK, DN_DV), F32)],
        compiler_params=_cp(("parallel", "arbitrary")),
        name="gdn_bwd" if reverse else "gdn_fwd",
    )(qkv, qkv, qkv, proj, gates_t, par, parc)[0]


def _headwise_rms(x, gain, width):
    parts = []
    for h0 in range(0, x.shape[-1], width):
        xh = x[:, h0:h0 + width]
        parts.append((xh * lax.rsqrt(jnp.mean(xh * xh, axis=-1, keepdims=True) + EPS)) * gain)
    return parts


def _outproj_kernel(*refs, even):
    if even:
        (a_ref, hf_ref, hb_ref, og_ref, gain_ref, x_ref, mod_ref, g2_ref, w_ref, rw_ref, rb_ref,
         xo_ref, hm_ref, tv_ref, ti_ref) = refs
        hs = _headwise_rms(hf_ref[0] + hb_ref[0], gain_ref[...], ML_DV)
        og = og_ref[0]
        mix = [a_ref[0]] + [hs[h] * jax.nn.sigmoid(og[:, h * ML_DV:(h + 1) * ML_DV]) for h in range(ML_HEADS)]
    else:
        (of_ref, ob_ref, z_ref, gain_ref, x_ref, mod_ref, g2_ref, w_ref, rw_ref, rb_ref,
         xo_ref, hm_ref, tv_ref, ti_ref) = refs
        hs = _headwise_rms(of_ref[0] + ob_ref[0], gain_ref[...], DN_DV)
        z = z_ref[0]
        mix = [hs[h] * _silu(z[:, h * DN_DV:(h + 1) * DN_DV]) for h in range(DN_HEADS)]
    mix = jnp.concatenate(mix, axis=-1)
    x_new = x_ref[0] + mod_ref[2:3, :] * _mm(mix, w_ref[...])
    xo_ref[0] = x_new
    hm = _norm_mod(x_new, g2_ref[...], mod_ref[3:4, :], mod_ref[4:5, :])
    hm_ref[0] = hm.astype(hm_ref.dtype)
    logits = jnp.dot(hm, rw_ref[...], preferred_element_type=F32, precision=lax.Precision.HIGHEST) + rb_ref[...]
    lane = lax.broadcasted_iota(jnp.int32, logits.shape, 1)
    vals = jnp.zeros_like(logits)
    idxs = jnp.zeros(logits.shape, jnp.int32)
    work = jnp.where(lane < N_EXPERTS, logits, -jnp.inf)
    for kk in range(TOP_K):
        mx = jnp.max(work, axis=-1, keepdims=True)
        am = jnp.min(jnp.where(work == mx, lane, LANES), axis=-1, keepdims=True)
        vals = jnp.where(lane == kk, mx, vals)
        idxs = jnp.where(lane == kk, am, idxs)
        work = jnp.where(lane == am, -jnp.inf, work)
    ex = jnp.where(lane < TOP_K, jnp.exp(vals - jnp.max(jnp.where(lane < TOP_K, vals, -jnp.inf), axis=-1, keepdims=True)), 0.0)
    tv_ref[0] = ex / jnp.sum(ex, axis=-1, keepdims=True)
    ti_ref[0] = idxs


def out_projection(even, mixer_in, gain, x, mod, g2, w_out, router_w, router_b):
    b, t, d = x.shape
    tok = lambda wdt, blk: pl.BlockSpec((1, TM, wdt), lambda i, j: (i, j, blk))
    if even:
        a, hf, hb, proj = mixer_in
        ins = [a, hf, hb, proj]
        specs = [tok(NA_W, 0), tok(ML_V_W, 0), tok(ML_V_W, 0), tok(ML_V_W, E_MO // ML_V_W)]
    else:
        of, ob, proj = mixer_in
        ins = [of, ob, proj]
        specs = [tok(DN_V_W, 0), tok(DN_V_W, 0), tok(DN_V_W, O_Z // DN_V_W)]
    rw = jnp.zeros((d, LANES), F32).at[:, :N_EXPERTS].set(router_w.astype(F32))
    rb = jnp.zeros((1, LANES), F32).at[0, :N_EXPERTS].set(router_b.astype(F32))
    const = lambda shape: pl.BlockSpec(shape, lambda i, j: (0,) * len(shape))
    ins += [gain.astype(F32).reshape(1, LANES), x, mod, g2.reshape(1, d), w_out.astype(MXU_DT), rw, rb]
    specs += [const((1, LANES)), tok(d, 0),
              pl.BlockSpec((None, None, 6, d), lambda i, j: (i, jnp.minimum(j, 1), 0, 0)),
              const((1, d)), const(w_out.shape), const((d, LANES)), const((1, LANES))]
    return pl.pallas_call(
        functools.partial(_outproj_kernel, even=even),
        grid=(b, t // TM),
        in_specs=specs,
        out_specs=[tok(d, 0), tok(d, 0), tok(LANES, 0), tok(LANES, 0)],
        out_shape=[jax.ShapeDtypeStruct((b, t, d), F32), jax.ShapeDtypeStruct((b, t, d), MXU_DT),
                   jax.ShapeDtypeStruct((b, t, LANES), F32), jax.ShapeDtypeStruct((b, t, LANES), jnp.int32)],
        compiler_params=_cp(("parallel", "arbitrary")),
        name="out_projection_even" if even else "out_projection_odd",
    )(*ins)


MOE_GRP = 2 * LANES


def _swiglu_perm():
    p = np.zeros((MOE_GRP, MOE_GRP), np.float32)
    j = np.arange(LANES)
    p[2 * j, j] = 1.0
    p[2 * j + 1, LANES + j] = 1.0
    return p


def _swiglu_col_order(n):
    g = np.arange(n // MOE_GRP)[:, None] * MOE_GRP
    j = np.arange(LANES)[None, :]
    return np.concatenate([g + 2 * j, g + 2 * j + 1], axis=1).reshape(-1)


def _experts_kernel(be_ref, nb_ref, x_ref, w1_ref, b1_ref, w2_ref, b2_ref, perm_ref, o_ref, w1s_ref, w2s_ref):
    i = pl.program_id(0)
    n_grp = w1s_ref.shape[1] // MOE_GRP

    @pl.when((i == 0) | (be_ref[i] != be_ref[jnp.maximum(i - 1, 0)]))
    def _():
        for g in range(n_grp):
            cols = slice(g * MOE_GRP, (g + 1) * MOE_GRP)
            w1s_ref[:, cols] = jnp.dot(w1_ref[0, :, cols].astype(MXU_DT), perm_ref[...],
                                       preferred_element_type=F32).astype(MXU_DT)
        w2s_ref[...] = w2_ref[0].astype(MXU_DT)

    @pl.when(i < nb_ref[0])
    def _():
        x = x_ref[...]
        acts = []
        for g in range(n_grp):
            cols = slice(g * MOE_GRP, (g + 1) * MOE_GRP)
            hid = jnp.dot(x, w1s_ref[:, cols], preferred_element_type=F32) + b1_ref[0, :, cols]
            glu = jnp.minimum(hid[:, :LANES], SWIGLU_LIMIT)
            lin = jnp.clip(hid[:, LANES:], -SWIGLU_LIMIT, SWIGLU_LIMIT)
            acts.append((glu * jax.nn.sigmoid(SWIGLU_ALPHA * glu) * (lin + 1.0)).astype(MXU_DT))
        act = jnp.concatenate(acts, axis=-1)
        o_ref[...] = (jnp.dot(act, w2s_ref[...], preferred_element_type=F32) + b2_ref[0]).astype(o_ref.dtype)

    @pl.when(i >= nb_ref[0])
    def _():
        o_ref[...] = jnp.zeros_like(o_ref)


def experts_ffn(xs, block_exp, n_used, w1, b1p, w2, b2):
    n_slots, d = xs.shape
    n_blocks = n_slots // MOE_BM
    e, _, dh2 = w1.shape
    wspec = lambda shape: pl.BlockSpec((1,) + shape, lambda i, be, nb: (be[i], 0, 0))
    return pl.pallas_call(
        _experts_kernel,
        grid_spec=pltpu.PrefetchScalarGridSpec(
            num_scalar_prefetch=2,
            grid=(n_blocks,),
            in_specs=[pl.BlockSpec((MOE_BM, d), lambda i, be, nb: (jnp.minimum(i, nb[0] - 1), 0)),
                      wspec((d, dh2)), wspec((1, dh2)), wspec((dh2 // 2, d)), wspec((1, d)),
                      pl.BlockSpec((MOE_GRP, MOE_GRP), lambda i, be, nb: (0, 0))],
            out_specs=pl.BlockSpec((MOE_BM, d), lambda i, be, nb: (i, 0)),
            scratch_shapes=[pltpu.VMEM((d, dh2), MXU_DT), pltpu.VMEM((dh2 // 2, d), MXU_DT)],
        ),
        out_shape=jax.ShapeDtypeStruct((n_slots, d), MXU_DT),
        compiler_params=_cp(("arbitrary",)),
        name="experts_ffn",
    )(block_exp, n_used, xs, w1, b1p, w2, b2, jnp.asarray(_swiglu_perm(), MXU_DT))


def _combine_kernel(x_ref, y_ref, gate_ref, mod_ref, o_ref):
    gate = gate_ref[0]
    d = x_ref.shape[-1]
    f = y_ref[0, :, 0:d].astype(F32) * gate[:, 0:1]
    for kk in range(1, TOP_K):
        f = f + y_ref[0, :, kk * d:(kk + 1) * d].astype(F32) * gate[:, kk:kk + 1]
    o_ref[0] = x_ref[0] + mod_ref[5:6, :] * f


def moe_combine(x, yg, gate, mod):
    b, t, d = x.shape
    return pl.pallas_call(
        _combine_kernel,
        grid=(b, t // TM),
        in_specs=[pl.BlockSpec((1, TM, d), lambda i, j: (i, j, 0)),
                  pl.BlockSpec((1, TM, TOP_K * d), lambda i, j: (i, j, 0)),
                  pl.BlockSpec((1, TM, LANES), lambda i, j: (i, j, 0)),
                  pl.BlockSpec((None, None, 6, d), lambda i, j: (i, jnp.minimum(j, 1), 0, 0))],
        out_specs=pl.BlockSpec((1, TM, d), lambda i, j: (i, j, 0)),
        out_shape=jax.ShapeDtypeStruct((b, t, d), F32),
        compiler_params=_cp(("parallel", "arbitrary")),
        name="moe_combine",
    )(x, yg, gate, mod)


def moe_layer(x_new, hm, gate, top_idx, mod, w1, b1, w2, b2):
    b, t, d = x_new.shape
    n_tok = b * t
    n_asg = n_tok * TOP_K
    flat_exp = top_idx[:, :, :TOP_K].reshape(n_asg)
    cb = 256
    onehot = (flat_exp[:, None] == jnp.arange(N_EXPERTS, dtype=jnp.int32)[None, :]).astype(F32)
    onehot = onehot.reshape(n_asg // cb, cb, N_EXPERTS)
    within = jnp.einsum('ij,bjk->bik', jnp.tril(jnp.ones((cb, cb), F32)), onehot)
    blk_tot = within[:, -1, :]
    blk_off = jnp.cumsum(blk_tot, axis=0) - blk_tot
    counts = (blk_off[-1] + blk_tot[-1]).astype(jnp.int32)
    rank = jnp.sum((within - onehot + blk_off[:, None, :]) * onehot, axis=-1).reshape(n_asg).astype(jnp.int32)
    padded = (counts + MOE_BM - 1) // MOE_BM * MOE_BM
    pad_end = jnp.cumsum(padded)
    dest = (pad_end - padded)[flat_exp] + rank
    n_blocks = -(-n_asg // MOE_BM) + N_EXPERTS
    n_slots = n_blocks * MOE_BM
    slot_tok = jnp.zeros((n_slots,), jnp.int32).at[dest].set(jnp.arange(n_asg, dtype=jnp.int32) // TOP_K)
    block_exp = jnp.minimum(jnp.searchsorted(pad_end, jnp.arange(n_blocks, dtype=jnp.int32) * MOE_BM, side='right'),
                            N_EXPERTS - 1).astype(jnp.int32)
    n_used = (pad_end[-1] // MOE_BM).astype(jnp.int32).reshape(1)
    xs = hm.reshape(n_tok, d)[slot_tok]
    b1p = b1.astype(F32)[:, _swiglu_col_order(b1.shape[-1])][:, None, :]
    y = experts_ffn(xs, block_exp, n_used, w1, b1p, w2, b2[:, None, :].astype(F32))
    yg = y[dest].reshape(b, t, TOP_K * d)
    return moe_combine(x_new, yg, gate, mod)


def _pad_cols(w, n):
    return jnp.pad(w, ((0, 0), (0, n - w.shape[1]))).astype(MXU_DT)


def kernel(x, c, ctx, c_ctx, ada_w, ada_b, norm1_g, norm2_g, ev_w_in, ev_w_out, na_q_gain, na_k_gain, na_rpb,
           ml_gate_b, ml_out_gain, od_w_in, od_w_out, dn_conv_w, dn_a_log, dn_dt_bias, dn_out_gain, router_w,
           router_b, exp_w1, exp_b1, exp_w2, exp_b2):
    b, s, d = x.shape
    depth = ada_w.shape[0]
    t = CTX_LEN + s
    xs = jnp.concatenate([ctx, x], axis=1)
    cond = jnp.zeros((16, d), F32).at[:b].set(c).at[b].set(c_ctx)
    mods = adaln_all(cond, ada_w, ada_b)
    mod_lat = mods[:, :b].reshape(depth, b, 1, 6, d)
    mod_ctx = jnp.broadcast_to(mods[:, b].reshape(depth, 1, 1, 6, d), (depth, b, 1, 6, d))
    mod_all = jnp.concatenate([mod_ctx, mod_lat], axis=2)
    cos, sin = rope_tables(t)
    for layer in range(depth):
        mod = mod_all[layer]
        if layer % 2 == 0:
            e = layer // 2
            proj = in_projection(xs, mod, norm1_g[layer], _pad_cols(ev_w_in[e], EVEN_PAD), 640)
            a = neighbourhood_attention(proj, na_q_gain[e], na_k_gain[e], na_bias_tables(na_rpb[e]))
            gates_t = proj[:, :, E_MG:E_MG + 16].transpose(0, 2, 1)
            hf = mlstm_direction(proj, gates_t, ml_gate_b[e], cos, sin, False)
            hb = mlstm_direction(proj, gates_t, ml_gate_b[e], cos, sin, True)
            mixer_in, gain, w_out = (a, hf, hb, proj), ml_out_gain[e], ev_w_out[e]
        else:
            o = layer // 2
            proj = in_projection(xs, mod, norm1_g[layer], _pad_cols(od_w_in[o], ODD_PAD), 1056)
            qkv = dn_conv(proj, dn_conv_w[o])
            gates_t = proj[:, :, O_A:O_A + 16].transpose(0, 2, 1)
            of = gdn_direction(qkv, proj, gates_t, dn_a_log[o], dn_dt_bias[o], False)
            ob = gdn_direction(qkv, proj, gates_t, dn_a_log[o], dn_dt_bias[o], True)
            mixer_in, gain, w_out = (of, ob, proj), dn_out_gain[o], od_w_out[o]
        x_new, hm, gate, top_idx = out_projection(layer % 2 == 0, mixer_in, gain, xs, mod, norm2_g[layer], w_out,
                                                  router_w[layer], router_b[layer])
        xs = moe_layer(x_new, hm, gate, top_idx, mod, exp_w1[layer], exp_b1[layer], exp_w2[layer], exp_b2[layer])
    return xs[:, CTX_LEN:]
```

```python
import functools

import numpy as np
import jax
import jax.numpy as jnp
from jax import lax
from jax.experimental import pallas as pl
from jax.experimental.pallas import tpu as pltpu

F32 = jnp.float32
MXU_DT = jnp.bfloat16
EPS = 1e-6
NEG = -1e30

D_MODEL = 1024
GRID_W = 64
CTX_LEN = 256
TM = 256
LANES = 128

NA_HEADS, NA_DH = 8, 64
NA_W = NA_HEADS * NA_DH
NA_WIN_R, NA_WIN_C = 8, 16
NA_RB = 4
NA_QB = NA_RB * GRID_W
NA_KROWS = NA_RB + NA_WIN_R - 1
NA_KU = NA_KROWS * GRID_W

ML_HEADS, ML_DQK, ML_DV, ML_CHUNK = 4, 64, 128, 64
ML_QK_W, ML_V_W = ML_HEADS * ML_DQK, ML_HEADS * ML_DV
ROPE_THETA = 10000.0
EVEN_IN = 3 * NA_W + 2 * ML_QK_W + 2 * ML_V_W + 4 * ML_HEADS
EVEN_PAD = 3200
E_Q, E_K, E_V = 0, NA_W, 2 * NA_W
E_MQ = 3 * NA_W
E_MK = E_MQ + ML_QK_W
E_MV = E_MK + ML_QK_W
E_MO = E_MV + ML_V_W
E_MG = E_MO + ML_V_W

DN_HEADS, DN_DK, DN_DV, DN_CHUNK, DN_CONV = 8, 128, 128, 64, 5
DN_QK_W, DN_V_W = DN_HEADS * DN_DK, DN_HEADS * DN_DV
DN_CONV_CH = 2 * DN_QK_W + DN_V_W
ODD_IN = DN_CONV_CH + DN_V_W + 4 * DN_HEADS
ODD_PAD = 4224
O_Z = DN_CONV_CH
O_A = O_Z + DN_V_W

N_EXPERTS, TOP_K, D_EXPERT = 32, 4, 1024
SWIGLU_LIMIT, SWIGLU_ALPHA = 7.0, 1.702
MOE_BM = 512

VMEM_LIMIT = 56 * 1024 * 1024


def _cp(sem):
    return pltpu.CompilerParams(dimension_semantics=sem, vmem_limit_bytes=VMEM_LIMIT)


def _mm(a, b):
    return jnp.dot(a.astype(MXU_DT), b.astype(MXU_DT), preferred_element_type=F32)


def _mm_nt(a, b):
    return lax.dot_general(a.astype(MXU_DT), b.astype(MXU_DT), (((1,), (1,)), ((), ())),
                           preferred_element_type=F32)


def _mm_tn(a, b):
    return lax.dot_general(a.astype(MXU_DT), b.astype(MXU_DT), (((0,), (0,)), ((), ())),
                           preferred_element_type=F32)


def _bmm(a, b):
    return jnp.einsum('bij,bjk->bik', a.astype(MXU_DT), b.astype(MXU_DT), preferred_element_type=F32)


def _bmm_nt(a, b):
    return jnp.einsum('bik,bjk->bij', a.astype(MXU_DT), b.astype(MXU_DT), preferred_element_type=F32)


def _bmm_tn(a, b):
    return jnp.einsum('bki,bkj->bij', a.astype(MXU_DT), b.astype(MXU_DT), preferred_element_type=F32)


def _silu(x):
    return x * jax.nn.sigmoid(x)


def _adaln_kernel(c_ref, w_ref, b_ref, o_ref):
    o_ref[0] = _mm(_silu(c_ref[...]), w_ref[0]) + b_ref[0]


def adaln_all(cond, ada_w, ada_b):
    depth, d, n = ada_w.shape
    tn = 1536
    return pl.pallas_call(
        _adaln_kernel,
        grid=(depth, n // tn),
        in_specs=[pl.BlockSpec(cond.shape, lambda l, j: (0, 0)),
                  pl.BlockSpec((1, d, tn), lambda l, j: (l, 0, j)),
                  pl.BlockSpec((1, 1, tn), lambda l, j: (l, 0, j))],
        out_specs=pl.BlockSpec((1, cond.shape[0], tn), lambda l, j: (l, 0, j)),
        out_shape=jax.ShapeDtypeStruct((depth, cond.shape[0], n), F32),
        compiler_params=_cp(("arbitrary", "arbitrary")),
        name="adaln",
    )(cond, ada_w, ada_b.reshape(depth, 1, n))


def _norm_mod(x, gain, shift, scale):
    y = x * lax.rsqrt(jnp.mean(x * x, axis=-1, keepdims=True) + EPS)
    return (y * gain) * (1.0 + scale) + shift


def _inproj_kernel(x_ref, mod_ref, g_ref, w_ref, o_ref, *, n_chunk):
    h = _norm_mod(x_ref[0], g_ref[...], mod_ref[0:1, :], mod_ref[1:2, :]).astype(MXU_DT)
    for n0 in range(0, o_ref.shape[-1], n_chunk):
        o_ref[0, :, n0:n0 + n_chunk] = jnp.dot(h, w_ref[:, n0:n0 + n_chunk], preferred_element_type=F32)


def in_projection(x, mod, gain, w, n_chunk):
    b, t, d = x.shape
    n = w.shape[1]
    return pl.pallas_call(
        functools.partial(_inproj_kernel, n_chunk=n_chunk),
        grid=(b, t // TM),
        in_specs=[pl.BlockSpec((1, TM, d), lambda i, j: (i, j, 0)),
                  pl.BlockSpec((None, None, 6, d), lambda i, j: (i, jnp.minimum(j, 1), 0, 0)),
                  pl.BlockSpec((1, d), lambda i, j: (0, 0)),
                  pl.BlockSpec((d, n), lambda i, j: (0, 0))],
        out_specs=pl.BlockSpec((1, TM, n), lambda i, j: (i, j, 0)),
        out_shape=jax.ShapeDtypeStruct((b, t, n), F32),
        compiler_params=_cp(("parallel", "arbitrary")),
        name="in_projection",
    )(x, mod, gain.reshape(1, d), w)


def _na_block_start(j):
    return jnp.clip(NA_RB * (j - 1) - NA_WIN_R // 2, 0, GRID_W - NA_KROWS)


def na_bias_tables(rpb):
    rows = GRID_W
    n_dr = 2 * NA_WIN_R - 1
    cq = np.arange(GRID_W)[:, None]
    ck = np.arange(GRID_W)[None, :]
    c0 = np.clip(cq - NA_WIN_C // 2, 0, GRID_W - NA_WIN_C)
    ok_c = (ck >= c0) & (ck < c0 + NA_WIN_C)
    ic = np.clip(ck - cq + NA_WIN_C - 1, 0, 2 * NA_WIN_C - 2)
    tiles = jnp.where(ok_c[None, None], rpb.astype(F32)[:, :, ic], NEG)
    tiles = jnp.concatenate([tiles, jnp.full_like(tiles[:, :1], NEG)], axis=1)
    idx = np.full((4, NA_RB, NA_KROWS), n_dr, np.int32)
    for var, rb in enumerate((0, 1, rows // NA_RB - 1)):
        start = int(np.clip(NA_RB * rb - NA_WIN_R // 2, 0, rows - NA_KROWS))
        for qi in range(NA_RB):
            rq = NA_RB * rb + qi
            r0 = int(np.clip(rq - NA_WIN_R // 2, 0, rows - NA_WIN_R))
            for kj in range(NA_KROWS):
                rk = start + kj
                if r0 <= rk < r0 + NA_WIN_R:
                    idx[var, qi, kj] = rk - rq + NA_WIN_R - 1
    full = tiles[:, idx]
    return full.transpose(0, 1, 2, 4, 3, 5).reshape(rpb.shape[0], 4, NA_QB, NA_KU)


def _na_kernel(q_ref, k_ref, v_ref, qg_ref, kg_ref, bias_ref, o_ref, kn_ref):
    j = pl.program_id(2)
    is_h0 = lax.broadcasted_iota(jnp.int32, (1, LANES), 1) < NA_DH

    def headnorm(x, gain):
        x2 = x * x
        s0 = jnp.sum(jnp.where(is_h0, x2, 0.0), axis=-1, keepdims=True)
        s1 = jnp.sum(jnp.where(is_h0, 0.0, x2), axis=-1, keepdims=True)
        ms = jnp.where(is_h0, s0, s1) * (1.0 / NA_DH)
        return (x * lax.rsqrt(ms + EPS)) * gain

    @pl.when(j == 0)
    def _():
        kn_ref[...] = headnorm(k_ref[0], kg_ref[...]).astype(MXU_DT)

    qn = headnorm(q_ref[0], qg_ref[...]) * (NA_DH ** -0.5)
    off = pl.multiple_of(CTX_LEN + _na_block_start(j) * GRID_W, GRID_W)
    k_loc = kn_ref[pl.ds(off, NA_KU), :]
    k_ctx = kn_ref[0:CTX_LEN, :]
    v_loc = v_ref[0, pl.ds(off, NA_KU), :].astype(MXU_DT)
    v_ctx = v_ref[0, 0:CTX_LEN, :].astype(MXU_DT)
    outs = []
    for h in range(2):
        qm = jnp.where(is_h0 if h == 0 else jnp.logical_not(is_h0), qn, 0.0)
        s_loc = _mm_nt(qm, k_loc) + bias_ref[h, 0]
        s_ctx = _mm_nt(qm, k_ctx)
        m = jnp.maximum(jnp.max(s_loc, axis=-1, keepdims=True), jnp.max(s_ctx, axis=-1, keepdims=True))
        p_loc = jnp.exp(s_loc - m)
        p_ctx = jnp.exp(s_ctx - m)
        den = jnp.sum(p_loc, axis=-1, keepdims=True) + jnp.sum(p_ctx, axis=-1, keepdims=True)
        outs.append((_mm(p_loc, v_loc) + _mm(p_ctx, v_ctx)) / den)
    o_ref[0] = jnp.where(is_h0, outs[0], outs[1])


def neighbourhood_attention(proj, q_gain, k_gain, bias):
    b, t, _ = proj.shape
    nq = t // NA_QB
    n_rb = GRID_W // NA_RB

    def bias_idx(i, hp, j):
        var = jnp.where(j == 0, 3, jnp.where(j == 1, 0, jnp.where(j == n_rb, 2, 1)))
        return (hp, var, 0, 0)

    tile2 = lambda g: jnp.tile(g.astype(F32), 2).reshape(1, LANES)
    return pl.pallas_call(
        _na_kernel,
        grid=(b, NA_HEADS // 2, nq),
        in_specs=[pl.BlockSpec((1, NA_QB, LANES), lambda i, hp, j: (i, j, E_Q // LANES + hp)),
                  pl.BlockSpec((1, t, LANES), lambda i, hp, j: (i, 0, E_K // LANES + hp)),
                  pl.BlockSpec((1, t, LANES), lambda i, hp, j: (i, 0, E_V // LANES + hp)),
                  pl.BlockSpec((1, LANES), lambda i, hp, j: (0, 0)),
                  pl.BlockSpec((1, LANES), lambda i, hp, j: (0, 0)),
                  pl.BlockSpec((2, 1, NA_QB, NA_KU), bias_idx)],
        out_specs=pl.BlockSpec((1, NA_QB, LANES), lambda i, hp, j: (i, j, hp)),
        out_shape=jax.ShapeDtypeStruct((b, t, NA_W), F32),
        scratch_shapes=[pltpu.VMEM((t, LANES), MXU_DT)],
        compiler_params=_cp(("parallel", "arbitrary", "arbitrary")),
        name="neighbourhood_attention",
    )(proj, proj, proj, tile2(q_gain), tile2(k_gain), bias)


def _scan_tile(i, n_tiles, reverse):
    if not reverse:
        return i
    return jnp.where(i == 0, 0, n_tiles - i)


def _tri_masks(n, reverse):
    r = lax.broadcasted_iota(jnp.int32, (n, n), 0)
    c = lax.broadcasted_iota(jnp.int32, (n, n), 1)
    if reverse:
        return c >= r, c > r, r >= c
    return c <= r, c < r, r <= c


def _cumsum_both(x_col, x_row, incl, incl_t):
    cum_col = jnp.sum(jnp.where(incl, x_row, 0.0), axis=1, keepdims=True)
    cum_row = jnp.sum(jnp.where(incl_t, x_col, 0.0), axis=0, keepdims=True)
    return cum_col, cum_row


def _rope(x, cos, sin):
    w = x.shape[-1]
    lane = lax.broadcasted_iota(jnp.int32, (1, w), 1)
    first = (lane % ML_DQK) < ML_DQK // 2
    swapped = jnp.where(first, pltpu.roll(x, w - ML_DQK // 2, 1), pltpu.roll(x, ML_DQK // 2, 1))
    return x * cos + swapped * sin


def _mlstm_kernel(q_ref, k_ref, v_ref, g_ref, gt_ref, gbr_ref, gbc_ref, cos_ref, sin_ref, o_ref,
                  c_ref, n_ref, m_ref, *, reverse):
    i = pl.program_id(1)
    d = 1 if reverse else 0

    @pl.when(i == 0)
    def _():
        c_ref[...] = jnp.zeros_like(c_ref)
        n_ref[...] = jnp.zeros_like(n_ref)
        m_ref[...] = jnp.zeros_like(m_ref)

    q = _rope(q_ref[0], cos_ref[...], sin_ref[...])
    k = _rope(k_ref[0] * (ML_DQK ** -0.5), cos_ref[...], sin_ref[...])
    g_col = g_ref[0] + gbr_ref[...]
    g_row = gt_ref[0] + gbc_ref[...]
    incl, _, incl_t = _tri_masks(ML_CHUNK, reverse)
    is_h0 = lax.broadcasted_iota(jnp.int32, (1, LANES), 1) < ML_DQK
    nc = TM // ML_CHUNK
    nh = ML_HEADS
    order = range(nc - 1, -1, -1) if reverse else range(nc)
    tokc = lambda ci: slice(ci * ML_CHUNK, (ci + 1) * ML_CHUNK)
    pair = lambda h: slice((h // 2) * LANES, (h // 2 + 1) * LANES)
    hmask = lambda h: is_h0 if h % 2 == 0 else jnp.logical_not(is_h0)
    stack = lambda fn: jnp.stack([fn(h, ci) for h in range(nh) for ci in range(nc)], axis=0)
    by_head = lambda a: a.reshape((nh, nc) + a.shape[1:])
    flat = lambda parts: jnp.stack(parts, axis=1).reshape((nh * nc,) + parts[0].shape[1:])
    qm = stack(lambda h, ci: jnp.where(hmask(h), q[tokc(ci), pair(h)], 0.0))
    km = stack(lambda h, ci: jnp.where(hmask(h), k[tokc(ci), pair(h)], 0.0))
    v = stack(lambda h, ci: v_ref[0, tokc(ci), h * ML_DV:(h + 1) * ML_DV])
    li_c = stack(lambda h, ci: g_col[tokc(ci), 8 * d + h:8 * d + h + 1])
    lf_c = jax.nn.log_sigmoid(stack(lambda h, ci: g_col[tokc(ci), 8 * d + 4 + h:8 * d + 5 + h]))
    li_r = stack(lambda h, ci: g_row[8 * d + h:8 * d + h + 1, tokc(ci)])
    lf_r = jax.nn.log_sigmoid(stack(lambda h, ci: g_row[8 * d + 4 + h:8 * d + 5 + h, tokc(ci)]))
    cum_c = jnp.sum(jnp.where(incl, lf_r, 0.0), axis=-1, keepdims=True)
    cum_r = jnp.sum(jnp.where(incl_t, lf_c, 0.0), axis=-2, keepdims=True)
    log_d = jnp.where(incl, cum_c - cum_r + li_r, NEG)
    m_loc = jnp.max(log_d, axis=-1, keepdims=True)
    cum_last = jnp.sum(lf_r, axis=-1, keepdims=True)
    log_w = cum_last - cum_c + li_c
    m_w = jnp.max(log_w, axis=-2, keepdims=True)
    cl4, mw4 = by_head(cum_last), by_head(m_w)
    m = m_ref[...][:, :, 0:1]
    m_prev, m_new = [None] * nc, [None] * nc
    for ci in order:
        m_prev[ci] = m
        m = jnp.maximum(cl4[:, ci] + m, mw4[:, ci])
        m_new[ci] = m
    m_ref[...] = jnp.broadcast_to(m, m_ref.shape)
    m_prev, m_new = flat(m_prev), flat(m_new)
    m_inter = cum_c + m_prev
    m_t = jnp.maximum(m_loc, m_inter)
    sg = _bmm_nt(qm, km) * jnp.exp(log_d - m_t)
    w_inter = jnp.exp(m_inter - m_t)
    kw = km * jnp.exp(log_w - m_new)
    upd_c, upd_n = by_head(_bmm_tn(kw, v)), by_head(jnp.sum(kw, axis=-2, keepdims=True))
    decay = by_head(jnp.exp(cum_last + m_prev - m_new))
    c_mat, n_vec = c_ref[...], n_ref[...]
    c_prev, n_prev = [None] * nc, [None] * nc
    for ci in order:
        c_prev[ci], n_prev[ci] = c_mat, n_vec
        c_mat = decay[:, ci] * c_mat + upd_c[:, ci]
        n_vec = decay[:, ci] * n_vec + upd_n[:, ci]
    c_ref[...] = c_mat
    n_ref[...] = n_vec
    num = _bmm(sg, v) + w_inter * _bmm(qm, flat(c_prev))
    den = jnp.sum(sg, axis=-1, keepdims=True) + w_inter * jnp.sum(qm * flat(n_prev), axis=-1, keepdims=True)
    h_out = by_head(num / jnp.maximum(jnp.abs(den), jnp.exp(-m_t)))
    for h in range(nh):
        for ci in range(nc):
            o_ref[0, 0, tokc(ci), h * ML_DV:(h + 1) * ML_DV] = h_out[h, ci]


def mlstm_direction(proj, gates_t, gate_b, cos, sin, reverse):
    b, t, _ = proj.shape
    n_tiles = t // TM
    tile = functools.partial(_scan_tile, n_tiles=n_tiles, reverse=reverse)
    gb = gate_b.astype(F32).reshape(16)
    gb_row = jnp.zeros((1, LANES), F32).at[0, :16].set(gb)
    gb_col = gb.reshape(16, 1)
    return pl.pallas_call(
        functools.partial(_mlstm_kernel, reverse=reverse),
        grid=(b, n_tiles),
        in_specs=[pl.BlockSpec((1, TM, ML_QK_W), lambda bi, i: (bi, tile(i), E_MQ // ML_QK_W)),
                  pl.BlockSpec((1, TM, ML_QK_W), lambda bi, i: (bi, tile(i), E_MK // ML_QK_W)),
                  pl.BlockSpec((1, TM, ML_V_W), lambda bi, i: (bi, tile(i), E_MV // ML_V_W)),
                  pl.BlockSpec((1, TM, LANES), lambda bi, i: (bi, tile(i), E_MG // LANES)),
                  pl.BlockSpec((1, 16, TM), lambda bi, i: (bi, 0, tile(i))),
                  pl.BlockSpec((1, LANES), lambda bi, i: (0, 0)),
                  pl.BlockSpec((16, 1), lambda bi, i: (0, 0)),
                  pl.BlockSpec((TM, ML_QK_W), lambda bi, i: (tile(i), 0)),
                  pl.BlockSpec((TM, ML_QK_W), lambda bi, i: (tile(i), 0))],
        out_specs=pl.BlockSpec((1, 1, TM, ML_V_W), lambda bi, i: (0, bi, tile(i), 0)),
        out_shape=jax.ShapeDtypeStruct((1, b, t, ML_V_W), F32),
        scratch_shapes=[pltpu.VMEM((ML_HEADS, LANES, ML_DV), F32),
                        pltpu.VMEM((ML_HEADS, 1, LANES), F32),
                        pltpu.VMEM((ML_HEADS, 1, LANES), F32)],
        compiler_params=_cp(("parallel", "arbitrary")),
        name="mlstm_bwd" if reverse else "mlstm_fwd",
    )(proj, proj, proj, proj, gates_t, gb_row, gb_col, cos, sin)[0]


def rope_tables(t):
    s = t - CTX_LEN
    tok = jnp.arange(s)
    n_freq = ML_DQK // 4
    inv = ROPE_THETA ** (-jnp.arange(n_freq, dtype=F32) / n_freq)
    ang = jnp.concatenate([(tok // GRID_W).astype(F32)[:, None] * inv,
                           (tok % GRID_W).astype(F32)[:, None] * inv], axis=-1)
    cos, sin = jnp.cos(ang), jnp.sin(ang)
    cos_h = jnp.concatenate([cos, cos], axis=-1)
    sin_h = jnp.concatenate([-sin, sin], axis=-1)
    cos_f = jnp.concatenate([jnp.ones((CTX_LEN, ML_DQK), F32), cos_h], axis=0)
    sin_f = jnp.concatenate([jnp.zeros((CTX_LEN, ML_DQK), F32), sin_h], axis=0)
    return jnp.tile(cos_f, (1, ML_HEADS)), jnp.tile(sin_f, (1, ML_HEADS))


def _dnconv_kernel(x_ref, w_ref, o_ref):
    cb = pl.program_id(1)
    x = x_ref[0]
    t = x.shape[0]
    tok = lax.broadcasted_iota(jnp.int32, (t, 1), 0)
    seg_lo = jnp.where(tok < CTX_LEN, 0, CTX_LEN)
    seg_hi = jnp.where(tok < CTX_LEN, CTX_LEN, t)
    acc = jnp.zeros_like(x)
    for j in range(DN_CONV):
        o = j - DN_CONV // 2
        xs = x if o == 0 else pltpu.roll(x, (-o) % t, 0)
        ok = (tok + o >= seg_lo) & (tok + o < seg_hi)
        acc = acc + jnp.where(ok, xs, 0.0) * w_ref[j:j + 1, :]
    y = _silu(acc)
    nrm = lax.rsqrt(jnp.sum(y * y, axis=-1, keepdims=True) + EPS)
    q_heads = DN_QK_W // LANES
    o_ref[0] = jnp.where(cb < q_heads, (y * nrm) * (DN_DK ** -0.5), jnp.where(cb < 2 * q_heads, y * nrm, y))


def dn_conv(proj, conv_w):
    b, t, _ = proj.shape
    return pl.pallas_call(
        _dnconv_kernel,
        grid=(b, DN_CONV_CH // LANES),
        in_specs=[pl.BlockSpec((1, t, LANES), lambda i, c: (i, 0, c)),
                  pl.BlockSpec((DN_CONV, LANES), lambda i, c: (0, c))],
        out_specs=pl.BlockSpec((1, t, LANES), lambda i, c: (i, 0, c)),
        out_shape=jax.ShapeDtypeStruct((b, t, DN_CONV_CH), F32),
        compiler_params=_cp(("parallel", "arbitrary")),
        name="dn_conv",
    )(proj, conv_w.astype(F32).T)


def _gdn_kernel(q_ref, k_ref, v_ref, ab_ref, at_ref, par_ref, parc_ref, o_ref, s_ref, *, reverse):
    i = pl.program_id(1)
    d = 1 if reverse else 0

    @pl.when(i == 0)
    def _():
        s_ref[...] = jnp.zeros_like(s_ref)

    incl, strict, incl_t = _tri_masks(DN_CHUNK, reverse)
    r = lax.broadcasted_iota(jnp.int32, (DN_CHUNK, DN_CHUNK), 0)
    c = lax.broadcasted_iota(jnp.int32, (DN_CHUNK, DN_CHUNK), 1)
    eye = (r == c).astype(F32)
    ab = ab_ref[0]
    g_all = -jnp.exp(par_ref[0:1, :]) * jax.nn.softplus(ab + par_ref[1:2, :])
    beta_full = jax.nn.sigmoid(ab)
    gt_all = -jnp.exp(parc_ref[:, 0:1]) * jax.nn.softplus(at_ref[0] + parc_ref[:, 1:2])

    nc = TM // DN_CHUNK
    nh = DN_HEADS
    tokc = lambda ci: slice(ci * DN_CHUNK, (ci + 1) * DN_CHUNK)
    hl = lambda h: slice(h * LANES, (h + 1) * LANES)
    stack = lambda fn: jnp.stack([fn(h, ci) for h in range(nh) for ci in range(nc)], axis=0)
    q = stack(lambda h, ci: q_ref[0, tokc(ci), hl(h)])
    k = stack(lambda h, ci: k_ref[0, tokc(ci), hl(h)])
    v = stack(lambda h, ci: v_ref[0, tokc(ci), hl(h)])
    g_c = stack(lambda h, ci: g_all[tokc(ci), 8 * d + h:8 * d + h + 1])
    beta = stack(lambda h, ci: beta_full[tokc(ci), 16 + 8 * d + h:17 + 8 * d + h])
    g_r = stack(lambda h, ci: gt_all[8 * d + h:8 * d + h + 1, tokc(ci)])
    gc_c = jnp.sum(jnp.where(incl, g_r, 0.0), axis=-1, keepdims=True)
    gc_r = jnp.sum(jnp.where(incl_t, g_c, 0.0), axis=-2, keepdims=True)
    decay = jnp.exp(jnp.where(incl, gc_c - gc_r, NEG))
    kb = k * beta
    a_mat = jnp.where(strict, _bmm_nt(kb, k) * decay, 0.0)
    pw = -a_mat
    t_inv = eye + pw
    for _ in range(5):
        pw = _bmm(pw, pw)
        t_inv = t_inv + _bmm(t_inv, pw)
    egc = jnp.exp(gc_c)
    uw = _bmm(t_inv, jnp.concatenate([v * beta, kb * egc], axis=-1))
    attn = _bmm_nt(q, k) * decay
    g_last = jnp.sum(g_c, axis=-2, keepdims=True)
    kg = k * jnp.exp(g_last - gc_c)
    ng = _bmm_tn(kg, uw)
    au = _bmm(attn, uw)
    lhs = jnp.concatenate([ng[:, :, DN_DV:], q * egc - au[:, :, DN_DV:]], axis=1)
    by_head = lambda a: a.reshape((nh, nc) + a.shape[1:])
    lhs, n_mat, o2, dec = by_head(lhs), by_head(ng[:, :, :DN_DV]), by_head(au[:, :, :DN_DV]), by_head(jnp.exp(g_last))
    s_mat = s_ref[...]
    for ci in (range(nc - 1, -1, -1) if reverse else range(nc)):
        prod = _bmm(lhs[:, ci], s_mat)
        o = prod[:, DN_DK:] + o2[:, ci]
        for h in range(nh):
            o_ref[0, 0, tokc(ci), hl(h)] = o[h]
        s_mat = s_mat * dec[:, ci] + n_mat[:, ci] - prod[:, :DN_DK]
    s_ref[...] = s_mat


def gdn_direction(qkv, proj, gates_t, a_log, dt_bias, reverse):
    b, t, _ = qkv.shape
    n_tiles = t // TM
    tile = functools.partial(_scan_tile, n_tiles=n_tiles, reverse=reverse)
    par = jnp.zeros((2, LANES), F32)
    par = par.at[0, :16].set(a_log.astype(F32).reshape(16)).at[1, :16].set(dt_bias.astype(F32).reshape(16))
    parc = par[:, :16].T
    return pl.pallas_call(
        functools.partial(_gdn_kernel, reverse=reverse),
        grid=(b, n_tiles),
        in_specs=[pl.BlockSpec((1, TM, DN_QK_W), lambda bi, i: (bi, tile(i), 0)),
                  pl.BlockSpec((1, TM, DN_QK_W), lambda bi, i: (bi, tile(i), 1)),
                  pl.BlockSpec((1, TM, DN_V_W), lambda bi, i: (bi, tile(i), 2)),
                  pl.BlockSpec((1, TM, LANES), lambda bi, i: (bi, tile(i), O_A // LANES)),
                  pl.BlockSpec((1, 16, TM), lambda bi, i: (bi, 0, tile(i))),
                  pl.BlockSpec((2, LANES), lambda bi, i: (0, 0)),
                  pl.BlockSpec((16, 2), lambda bi, i: (0, 0))],
        out_specs=pl.BlockSpec((1, 1, TM, DN_V_W), lambda bi, i: (0, bi, tile(i), 0)),
        out_shape=jax.ShapeDtypeStruct((1, b, t, DN_V_W), F32),
        scratch_shapes=[pltpu.VMEM((DN_HEADS, DN_DK, DN_DV), F32)],
        compiler_params=_cp(("parallel", "arbitrary")),
        name="gdn_bwd" if reverse else "gdn_fwd",
    )(qkv, qkv, qkv, proj, gates_t, par, parc)[0]


def _headwise_rms(x, gain, width):
    parts = []
    for h0 in range(0, x.shape[-1], width):
        xh = x[:, h0:h0 + width]
        parts.append((xh * lax.rsqrt(jnp.mean(xh * xh, axis=-1, keepdims=True) + EPS)) * gain)
    return parts


def _outproj_kernel(*refs, even):
    if even:
        (a_ref, hf_ref, hb_ref, og_ref, gain_ref, x_ref, mod_ref, g2_ref, w_ref, rw_ref, rb_ref,
         xo_ref, hm_ref, tv_ref, ti_ref) = refs
        hs = _headwise_rms(hf_ref[0] + hb_ref[0], gain_ref[...], ML_DV)
        og = og_ref[0]
        mix = [a_ref[0]] + [hs[h] * jax.nn.sigmoid(og[:, h * ML_DV:(h + 1) * ML_DV]) for h in range(ML_HEADS)]
    else:
        (of_ref, ob_ref, z_ref, gain_ref, x_ref, mod_ref, g2_ref, w_ref, rw_ref, rb_ref,
         xo_ref, hm_ref, tv_ref, ti_ref) = refs
        hs = _headwise_rms(of_ref[0] + ob_ref[0], gain_ref[...], DN_DV)
        z = z_ref[0]
        mix = [hs[h] * _silu(z[:, h * DN_DV:(h + 1) * DN_DV]) for h in range(DN_HEADS)]
    mix = jnp.concatenate(mix, axis=-1)
    x_new = x_ref[0] + mod_ref[2:3, :] * _mm(mix, w_ref[...])
    xo_ref[0] = x_new
    hm = _norm_mod(x_new, g2_ref[...], mod_ref[3:4, :], mod_ref[4:5, :])
    hm_ref[0] = hm.astype(hm_ref.dtype)
    logits = jnp.dot(hm, rw_ref[...], preferred_element_type=F32, precision=lax.Precision.HIGHEST) + rb_ref[...]
    lane = lax.broadcasted_iota(jnp.int32, logits.shape, 1)
    vals = jnp.zeros_like(logits)
    idxs = jnp.zeros(logits.shape, jnp.int32)
    work = jnp.where(lane < N_EXPERTS, logits, -jnp.inf)
    for kk in range(TOP_K):
        mx = jnp.max(work, axis=-1, keepdims=True)
        am = jnp.min(jnp.where(work == mx, lane, LANES), axis=-1, keepdims=True)
        vals = jnp.where(lane == kk, mx, vals)
        idxs = jnp.where(lane == kk, am, idxs)
        work = jnp.where(lane == am, -jnp.inf, work)
    ex = jnp.where(lane < TOP_K, jnp.exp(vals - jnp.max(jnp.where(lane < TOP_K, vals, -jnp.inf), axis=-1, keepdims=True)), 0.0)
    tv_ref[0] = ex / jnp.sum(ex, axis=-1, keepdims=True)
    ti_ref[0] = idxs


def out_projection(even, mixer_in, gain, x, mod, g2, w_out, router_w, router_b):
    b, t, d = x.shape
    tok = lambda wdt, blk: pl.BlockSpec((1, TM, wdt), lambda i, j: (i, j, blk))
    if even:
        a, hf, hb, proj = mixer_in
        ins = [a, hf, hb, proj]
        specs = [tok(NA_W, 0), tok(ML_V_W, 0), tok(ML_V_W, 0), tok(ML_V_W, E_MO // ML_V_W)]
    else:
        of, ob, proj = mixer_in
        ins = [of, ob, proj]
        specs = [tok(DN_V_W, 0), tok(DN_V_W, 0), tok(DN_V_W, O_Z // DN_V_W)]
    rw = jnp.zeros((d, LANES), F32).at[:, :N_EXPERTS].set(router_w.astype(F32))
    rb = jnp.zeros((1, LANES), F32).at[0, :N_EXPERTS].set(router_b.astype(F32))
    const = lambda shape: pl.BlockSpec(shape, lambda i, j: (0,) * len(shape))
    ins += [gain.astype(F32).reshape(1, LANES), x, mod, g2.reshape(1, d), w_out.astype(MXU_DT), rw, rb]
    specs += [const((1, LANES)), tok(d, 0),
              pl.BlockSpec((None, None, 6, d), lambda i, j: (i, jnp.minimum(j, 1), 0, 0)),
              const((1, d)), const(w_out.shape), const((d, LANES)), const((1, LANES))]
    return pl.pallas_call(
        functools.partial(_outproj_kernel, even=even),
        grid=(b, t // TM),
        in_specs=specs,
        out_specs=[tok(d, 0), tok(d, 0), tok(LANES, 0), tok(LANES, 0)],
        out_shape=[jax.ShapeDtypeStruct((b, t, d), F32), jax.ShapeDtypeStruct((b, t, d), MXU_DT),
                   jax.ShapeDtypeStruct((b, t, LANES), F32), jax.ShapeDtypeStruct((b, t, LANES), jnp.int32)],
        compiler_params=_cp(("parallel", "arbitrary")),
        name="out_projection_even" if even else "out_projection_odd",
    )(*ins)


MOE_GRP = 2 * LANES


def _swiglu_perm():
    p = np.zeros((MOE_GRP, MOE_GRP), np.float32)
    j = np.arange(LANES)
    p[2 * j, j] = 1.0
    p[2 * j + 1, LANES + j] = 1.0
    return p


def _swiglu_col_order(n):
    g = np.arange(n // MOE_GRP)[:, None] * MOE_GRP
    j = np.arange(LANES)[None, :]
    return np.concatenate([g + 2 * j, g + 2 * j + 1], axis=1).reshape(-1)


def _experts_kernel(be_ref, nb_ref, x_ref, w1_ref, b1_ref, w2_ref, b2_ref, perm_ref, o_ref, w1s_ref, w2s_ref):
    i = pl.program_id(0)
    n_grp = w1s_ref.shape[1] // MOE_GRP

    @pl.when((i == 0) | (be_ref[i] != be_ref[jnp.maximum(i - 1, 0)]))
    def _():
        for g in range(n_grp):
            cols = slice(g * MOE_GRP, (g + 1) * MOE_GRP)
            w1s_ref[:, cols] = jnp.dot(w1_ref[0, :, cols].astype(MXU_DT), perm_ref[...],
                                       preferred_element_type=F32).astype(MXU_DT)
        w2s_ref[...] = w2_ref[0].astype(MXU_DT)

    @pl.when(i < nb_ref[0])
    def _():
        x = x_ref[...]
        acts = []
        for g in range(n_grp):
            cols = slice(g * MOE_GRP, (g + 1) * MOE_GRP)
            hid = jnp.dot(x, w1s_ref[:, cols], preferred_element_type=F32) + b1_ref[0, :, cols]
            glu = jnp.minimum(hid[:, :LANES], SWIGLU_LIMIT)
            lin = jnp.clip(hid[:, LANES:], -SWIGLU_LIMIT, SWIGLU_LIMIT)
            acts.append((glu * jax.nn.sigmoid(SWIGLU_ALPHA * glu) * (lin + 1.0)).astype(MXU_DT))
        act = jnp.concatenate(acts, axis=-1)
        o_ref[...] = (jnp.dot(act, w2s_ref[...], preferred_element_type=F32) + b2_ref[0]).astype(o_ref.dtype)

    @pl.when(i >= nb_ref[0])
    def _():
        o_ref[...] = jnp.zeros_like(o_ref)


def experts_ffn(xs, block_exp, n_used, layer, w1, b1p, w2, b2):
    n_slots, d = xs.shape
    n_blocks = n_slots // MOE_BM
    dh2 = w1.shape[-1]
    wspec = lambda shape: pl.BlockSpec((None, 1) + shape, lambda i, be, nb: (layer, be[i], 0, 0))
    bspec = lambda shape: pl.BlockSpec((1,) + shape, lambda i, be, nb: (be[i], 0, 0))
    return pl.pallas_call(
        _experts_kernel,
        grid_spec=pltpu.PrefetchScalarGridSpec(
            num_scalar_prefetch=2,
            grid=(n_blocks,),
            in_specs=[pl.BlockSpec((MOE_BM, d), lambda i, be, nb: (jnp.minimum(i, nb[0] - 1), 0)),
                      wspec((d, dh2)), bspec((1, dh2)), wspec((dh2 // 2, d)), bspec((1, d)),
                      pl.BlockSpec((MOE_GRP, MOE_GRP), lambda i, be, nb: (0, 0))],
            out_specs=pl.BlockSpec((MOE_BM, d), lambda i, be, nb: (i, 0)),
            scratch_shapes=[pltpu.VMEM((d, dh2), MXU_DT), pltpu.VMEM((dh2 // 2, d), MXU_DT)],
        ),
        out_shape=jax.ShapeDtypeStruct((n_slots, d), MXU_DT),
        compiler_params=_cp(("arbitrary",)),
        name="experts_ffn",
    )(block_exp, n_used, xs, w1, b1p, w2, b2, jnp.asarray(_swiglu_perm(), MXU_DT))


def _combine_kernel(x_ref, y_ref, gate_ref, mod_ref, o_ref):
    gate = gate_ref[0]
    f = y_ref[0, 0].astype(F32) * gate[:, 0:1]
    for kk in range(1, TOP_K):
        f = f + y_ref[0, kk].astype(F32) * gate[:, kk:kk + 1]
    o_ref[0] = x_ref[0] + mod_ref[5:6, :] * f


def moe_combine(x, yg, gate, mod):
    b, t, d = x.shape
    return pl.pallas_call(
        _combine_kernel,
        grid=(b, t // TM),
        in_specs=[pl.BlockSpec((1, TM, d), lambda i, j: (i, j, 0)),
                  pl.BlockSpec((1, TOP_K, TM, d), lambda i, j: (i, 0, j, 0)),
                  pl.BlockSpec((1, TM, LANES), lambda i, j: (i, j, 0)),
                  pl.BlockSpec((None, None, 6, d), lambda i, j: (i, jnp.minimum(j, 1), 0, 0))],
        out_specs=pl.BlockSpec((1, TM, d), lambda i, j: (i, j, 0)),
        out_shape=jax.ShapeDtypeStruct((b, t, d), F32),
        compiler_params=_cp(("parallel", "arbitrary")),
        name="moe_combine",
    )(x, yg, gate, mod)


def moe_layer(x_new, hm, gate, top_idx, mod, layer, w1, b1, w2, b2):
    b, t, d = x_new.shape
    n_tok = b * t
    n_asg = n_tok * TOP_K
    flat_exp = top_idx[:, :, :TOP_K].transpose(0, 2, 1).reshape(n_asg)
    tok_of_asg = (jnp.arange(b, dtype=jnp.int32)[:, None, None] * t + jnp.zeros((1, TOP_K, 1), jnp.int32)
                  + jnp.arange(t, dtype=jnp.int32)[None, None, :]).reshape(n_asg)
    cb = 256
    onehot = (flat_exp[:, None] == jnp.arange(N_EXPERTS, dtype=jnp.int32)[None, :]).astype(F32)
    onehot = onehot.reshape(n_asg // cb, cb, N_EXPERTS)
    within = jnp.einsum('ij,bjk->bik', jnp.tril(jnp.ones((cb, cb), F32)), onehot)
    blk_tot = within[:, -1, :]
    blk_off = jnp.cumsum(blk_tot, axis=0) - blk_tot
    counts = (blk_off[-1] + blk_tot[-1]).astype(jnp.int32)
    rank = jnp.sum((within - onehot + blk_off[:, None, :]) * onehot, axis=-1).reshape(n_asg).astype(jnp.int32)
    padded = (counts + MOE_BM - 1) // MOE_BM * MOE_BM
    pad_end = jnp.cumsum(padded)
    dest = (pad_end - padded)[flat_exp] + rank
    n_blocks = -(-n_asg // MOE_BM) + N_EXPERTS
    n_slots = n_blocks * MOE_BM
    slot_tok = jnp.zeros((n_slots,), jnp.int32).at[dest].set(tok_of_asg, unique_indices=True,
                                                             mode='promise_in_bounds')
    block_start = jnp.arange(n_blocks, dtype=jnp.int32) * MOE_BM
    block_exp = jnp.minimum(jnp.sum((pad_end[None, :] <= block_start[:, None]).astype(jnp.int32), axis=1),
                            N_EXPERTS - 1)
    n_used = (pad_end[-1] // MOE_BM).astype(jnp.int32).reshape(1)
    xs = hm.reshape(n_tok, d)[slot_tok]
    b1p = b1.astype(F32)[:, _swiglu_col_order(b1.shape[-1])][:, None, :]
    y = experts_ffn(xs, block_exp, n_used, layer, w1, b1p, w2, b2[:, None, :].astype(F32))
    yg = y[dest].reshape(b, TOP_K, t, d)
    return moe_combine(x_new, yg, gate, mod)


def _pad_cols(w, n):
    return jnp.pad(w, ((0, 0), (0, n - w.shape[1]))).astype(MXU_DT)


def kernel(x, c, ctx, c_ctx, ada_w, ada_b, norm1_g, norm2_g, ev_w_in, ev_w_out, na_q_gain, na_k_gain, na_rpb,
           ml_gate_b, ml_out_gain, od_w_in, od_w_out, dn_conv_w, dn_a_log, dn_dt_bias, dn_out_gain, router_w,
           router_b, exp_w1, exp_b1, exp_w2, exp_b2):
    b, s, d = x.shape
    depth = ada_w.shape[0]
    t = CTX_LEN + s
    xs = jnp.concatenate([ctx, x], axis=1)
    cond = jnp.zeros((16, d), F32).at[:b].set(c).at[b].set(c_ctx)
    mods = adaln_all(cond, ada_w, ada_b)
    mod_lat = mods[:, :b].reshape(depth, b, 1, 6, d)
    mod_ctx = jnp.broadcast_to(mods[:, b].reshape(depth, 1, 1, 6, d), (depth, b, 1, 6, d))
    mod_all = jnp.concatenate([mod_ctx, mod_lat], axis=2)
    cos, sin = rope_tables(t)
    for layer in range(depth):
        mod = mod_all[layer]
        if layer % 2 == 0:
            e = layer // 2
            proj = in_projection(xs, mod, norm1_g[layer], _pad_cols(ev_w_in[e], EVEN_PAD), 640)
            a = neighbourhood_attention(proj, na_q_gain[e], na_k_gain[e], na_bias_tables(na_rpb[e]))
            gates_t = proj[:, :, E_MG:E_MG + 16].transpose(0, 2, 1)
            hf = mlstm_direction(proj, gates_t, ml_gate_b[e], cos, sin, False)
            hb = mlstm_direction(proj, gates_t, ml_gate_b[e], cos, sin, True)
            mixer_in, gain, w_out = (a, hf, hb, proj), ml_out_gain[e], ev_w_out[e]
        else:
            o = layer // 2
            proj = in_projection(xs, mod, norm1_g[layer], _pad_cols(od_w_in[o], ODD_PAD), 1056)
            qkv = dn_conv(proj, dn_conv_w[o])
            gates_t = proj[:, :, O_A:O_A + 16].transpose(0, 2, 1)
            of = gdn_direction(qkv, proj, gates_t, dn_a_log[o], dn_dt_bias[o], False)
            ob = gdn_direction(qkv, proj, gates_t, dn_a_log[o], dn_dt_bias[o], True)
            mixer_in, gain, w_out = (of, ob, proj), dn_out_gain[o], od_w_out[o]
        x_new, hm, gate, top_idx = out_projection(layer % 2 == 0, mixer_in, gain, xs, mod, norm2_g[layer], w_out,
                                                  router_w[layer], router_b[layer])
        xs = moe_layer(x_new, hm, gate, top_idx, mod, layer, exp_w1, exp_b1[layer], exp_w2, exp_b2[layer])
    return xs[:, CTX_LEN:]
```

```python
import functools

import numpy as np
import jax
import jax.numpy as jnp
from jax import lax
from jax.experimental import pallas as pl
from jax.experimental.pallas import tpu as pltpu

F32 = jnp.float32
MXU_DT = jnp.bfloat16
EPS = 1e-6
NEG = -1e30

D_MODEL = 1024
GRID_W = 64
CTX_LEN = 256
TM = 256
LANES = 128

NA_HEADS, NA_DH = 8, 64
NA_W = NA_HEADS * NA_DH
NA_WIN_R, NA_WIN_C = 8, 16
NA_RB = 4
NA_QB = NA_RB * GRID_W
NA_KROWS = NA_RB + NA_WIN_R - 1
NA_KU = NA_KROWS * GRID_W

ML_HEADS, ML_DQK, ML_DV, ML_CHUNK = 4, 64, 128, 64
ML_QK_W, ML_V_W = ML_HEADS * ML_DQK, ML_HEADS * ML_DV
ROPE_THETA = 10000.0
EVEN_IN = 3 * NA_W + 2 * ML_QK_W + 2 * ML_V_W + 4 * ML_HEADS
EVEN_PAD = 3200
E_Q, E_K, E_V = 0, NA_W, 2 * NA_W
E_MQ = 3 * NA_W
E_MK = E_MQ + ML_QK_W
E_MV = E_MK + ML_QK_W
E_MO = E_MV + ML_V_W
E_MG = E_MO + ML_V_W

DN_HEADS, DN_DK, DN_DV, DN_CHUNK, DN_CONV = 8, 128, 128, 64, 5
DN_QK_W, DN_V_W = DN_HEADS * DN_DK, DN_HEADS * DN_DV
DN_CONV_CH = 2 * DN_QK_W + DN_V_W
ODD_IN = DN_CONV_CH + DN_V_W + 4 * DN_HEADS
ODD_PAD = 4224
O_Z = DN_CONV_CH
O_A = O_Z + DN_V_W

N_EXPERTS, TOP_K, D_EXPERT = 32, 4, 1024
SWIGLU_LIMIT, SWIGLU_ALPHA = 7.0, 1.702
MOE_BM = 512

VMEM_LIMIT = 56 * 1024 * 1024


def _cp(sem):
    return pltpu.CompilerParams(dimension_semantics=sem, vmem_limit_bytes=VMEM_LIMIT)


def _mm(a, b):
    return jnp.dot(a.astype(MXU_DT), b.astype(MXU_DT), preferred_element_type=F32)


def _mm_nt(a, b):
    return lax.dot_general(a.astype(MXU_DT), b.astype(MXU_DT), (((1,), (1,)), ((), ())),
                           preferred_element_type=F32)


def _mm_tn(a, b):
    return lax.dot_general(a.astype(MXU_DT), b.astype(MXU_DT), (((0,), (0,)), ((), ())),
                           preferred_element_type=F32)


def _bmm(a, b):
    return jnp.einsum('bij,bjk->bik', a.astype(MXU_DT), b.astype(MXU_DT), preferred_element_type=F32)


def _bmm_nt(a, b):
    return jnp.einsum('bik,bjk->bij', a.astype(MXU_DT), b.astype(MXU_DT), preferred_element_type=F32)


def _bmm_tn(a, b):
    return jnp.einsum('bki,bkj->bij', a.astype(MXU_DT), b.astype(MXU_DT), preferred_element_type=F32)


def _silu(x):
    return x * jax.nn.sigmoid(x)


def _adaln_kernel(c_ref, w_ref, b_ref, o_ref):
    o_ref[0] = _mm(_silu(c_ref[...]), w_ref[0]) + b_ref[0]


def adaln_all(cond, ada_w, ada_b):
    depth, d, n = ada_w.shape
    tn = 1536
    return pl.pallas_call(
        _adaln_kernel,
        grid=(depth, n // tn),
        in_specs=[pl.BlockSpec(cond.shape, lambda l, j: (0, 0)),
                  pl.BlockSpec((1, d, tn), lambda l, j: (l, 0, j)),
                  pl.BlockSpec((1, 1, tn), lambda l, j: (l, 0, j))],
        out_specs=pl.BlockSpec((1, cond.shape[0], tn), lambda l, j: (l, 0, j)),
        out_shape=jax.ShapeDtypeStruct((depth, cond.shape[0], n), F32),
        compiler_params=_cp(("arbitrary", "arbitrary")),
        name="adaln",
    )(cond, ada_w, ada_b.reshape(depth, 1, n))


def _norm_mod(x, gain, shift, scale):
    y = x * lax.rsqrt(jnp.mean(x * x, axis=-1, keepdims=True) + EPS)
    return (y * gain) * (1.0 + scale) + shift


def _inproj_kernel(x_ref, mod_ref, g_ref, w_ref, o_ref, *, n_chunk):
    h = _norm_mod(x_ref[0], g_ref[...], mod_ref[0:1, :], mod_ref[1:2, :]).astype(MXU_DT)
    for n0 in range(0, o_ref.shape[-1], n_chunk):
        o_ref[0, :, n0:n0 + n_chunk] = jnp.dot(h, w_ref[:, n0:n0 + n_chunk], preferred_element_type=F32)


def in_projection(x, mod, gain, w, n_chunk):
    b, t, d = x.shape
    n = w.shape[1]
    return pl.pallas_call(
        functools.partial(_inproj_kernel, n_chunk=n_chunk),
        grid=(b, t // TM),
        in_specs=[pl.BlockSpec((1, TM, d), lambda i, j: (i, j, 0)),
                  pl.BlockSpec((None, None, 6, d), lambda i, j: (i, jnp.minimum(j, 1), 0, 0)),
                  pl.BlockSpec((1, d), lambda i, j: (0, 0)),
                  pl.BlockSpec((d, n), lambda i, j: (0, 0))],
        out_specs=pl.BlockSpec((1, TM, n), lambda i, j: (i, j, 0)),
        out_shape=jax.ShapeDtypeStruct((b, t, n), F32),
        compiler_params=_cp(("parallel", "arbitrary")),
        name="in_projection",
    )(x, mod, gain.reshape(1, d), w)


def _na_block_start(j):
    return jnp.clip(NA_RB * (j - 1) - NA_WIN_R // 2, 0, GRID_W - NA_KROWS)


def na_bias_tables(rpb):
    rows = GRID_W
    n_dr = 2 * NA_WIN_R - 1
    cq = np.arange(GRID_W)[:, None]
    ck = np.arange(GRID_W)[None, :]
    c0 = np.clip(cq - NA_WIN_C // 2, 0, GRID_W - NA_WIN_C)
    ok_c = (ck >= c0) & (ck < c0 + NA_WIN_C)
    ic = np.clip(ck - cq + NA_WIN_C - 1, 0, 2 * NA_WIN_C - 2)
    tiles = jnp.where(ok_c[None, None], rpb.astype(F32)[:, :, ic], NEG)
    tiles = jnp.concatenate([tiles, jnp.full_like(tiles[:, :1], NEG)], axis=1)
    idx = np.full((4, NA_RB, NA_KROWS), n_dr, np.int32)
    for var, rb in enumerate((0, 1, rows // NA_RB - 1)):
        start = int(np.clip(NA_RB * rb - NA_WIN_R // 2, 0, rows - NA_KROWS))
        for qi in range(NA_RB):
            rq = NA_RB * rb + qi
            r0 = int(np.clip(rq - NA_WIN_R // 2, 0, rows - NA_WIN_R))
            for kj in range(NA_KROWS):
                rk = start + kj
                if r0 <= rk < r0 + NA_WIN_R:
                    idx[var, qi, kj] = rk - rq + NA_WIN_R - 1
    full = tiles[:, idx]
    return full.transpose(0, 1, 2, 4, 3, 5).reshape(rpb.shape[0], 4, NA_QB, NA_KU)


def _na_kernel(q_ref, k_ref, v_ref, qg_ref, kg_ref, bias_ref, o_ref, kn_ref):
    j = pl.program_id(2)
    is_h0 = lax.broadcasted_iota(jnp.int32, (1, LANES), 1) < NA_DH

    def headnorm(x, gain):
        x2 = x * x
        s0 = jnp.sum(jnp.where(is_h0, x2, 0.0), axis=-1, keepdims=True)
        s1 = jnp.sum(jnp.where(is_h0, 0.0, x2), axis=-1, keepdims=True)
        ms = jnp.where(is_h0, s0, s1) * (1.0 / NA_DH)
        return (x * lax.rsqrt(ms + EPS)) * gain

    @pl.when(j == 0)
    def _():
        kn_ref[...] = headnorm(k_ref[0], kg_ref[...]).astype(MXU_DT)

    qn = headnorm(q_ref[0], qg_ref[...]) * (NA_DH ** -0.5)
    off = pl.multiple_of(CTX_LEN + _na_block_start(j) * GRID_W, GRID_W)
    k_loc = kn_ref[pl.ds(off, NA_KU), :]
    k_ctx = kn_ref[0:CTX_LEN, :]
    v_loc = v_ref[0, pl.ds(off, NA_KU), :].astype(MXU_DT)
    v_ctx = v_ref[0, 0:CTX_LEN, :].astype(MXU_DT)
    outs = []
    for h in range(2):
        qm = jnp.where(is_h0 if h == 0 else jnp.logical_not(is_h0), qn, 0.0)
        s_loc = _mm_nt(qm, k_loc) + bias_ref[h, 0]
        s_ctx = _mm_nt(qm, k_ctx)
        m = jnp.maximum(jnp.max(s_loc, axis=-1, keepdims=True), jnp.max(s_ctx, axis=-1, keepdims=True))
        p_loc = jnp.exp(s_loc - m)
        p_ctx = jnp.exp(s_ctx - m)
        den = jnp.sum(p_loc, axis=-1, keepdims=True) + jnp.sum(p_ctx, axis=-1, keepdims=True)
        outs.append((_mm(p_loc, v_loc) + _mm(p_ctx, v_ctx)) / den)
    o_ref[0] = jnp.where(is_h0, outs[0], outs[1])


def neighbourhood_attention(proj, q_gain, k_gain, bias):
    b, t, _ = proj.shape
    nq = t // NA_QB
    n_rb = GRID_W // NA_RB

    def bias_idx(i, hp, j):
        var = jnp.where(j == 0, 3, jnp.where(j == 1, 0, jnp.where(j == n_rb, 2, 1)))
        return (hp, var, 0, 0)

    tile2 = lambda g: jnp.tile(g.astype(F32), 2).reshape(1, LANES)
    return pl.pallas_call(
        _na_kernel,
        grid=(b, NA_HEADS // 2, nq),
        in_specs=[pl.BlockSpec((1, NA_QB, LANES), lambda i, hp, j: (i, j, E_Q // LANES + hp)),
                  pl.BlockSpec((1, t, LANES), lambda i, hp, j: (i, 0, E_K // LANES + hp)),
                  pl.BlockSpec((1, t, LANES), lambda i, hp, j: (i, 0, E_V // LANES + hp)),
                  pl.BlockSpec((1, LANES), lambda i, hp, j: (0, 0)),
                  pl.BlockSpec((1, LANES), lambda i, hp, j: (0, 0)),
                  pl.BlockSpec((2, 1, NA_QB, NA_KU), bias_idx)],
        out_specs=pl.BlockSpec((1, NA_QB, LANES), lambda i, hp, j: (i, j, hp)),
        out_shape=jax.ShapeDtypeStruct((b, t, NA_W), F32),
        scratch_shapes=[pltpu.VMEM((t, LANES), MXU_DT)],
        compiler_params=_cp(("parallel", "arbitrary", "arbitrary")),
        name="neighbourhood_attention",
    )(proj, proj, proj, tile2(q_gain), tile2(k_gain), bias)


def _scan_tile(i, n_tiles, reverse):
    if not reverse:
        return i
    return jnp.where(i == 0, 0, n_tiles - i)


def _tri_masks(n, reverse):
    r = lax.broadcasted_iota(jnp.int32, (n, n), 0)
    c = lax.broadcasted_iota(jnp.int32, (n, n), 1)
    if reverse:
        return c >= r, c > r, r >= c
    return c <= r, c < r, r <= c


def _cumsum_both(x_col, x_row, incl, incl_t):
    cum_col = jnp.sum(jnp.where(incl, x_row, 0.0), axis=1, keepdims=True)
    cum_row = jnp.sum(jnp.where(incl_t, x_col, 0.0), axis=0, keepdims=True)
    return cum_col, cum_row


def _rope(x, cos, sin):
    w = x.shape[-1]
    lane = lax.broadcasted_iota(jnp.int32, (1, w), 1)
    first = (lane % ML_DQK) < ML_DQK // 2
    swapped = jnp.where(first, pltpu.roll(x, w - ML_DQK // 2, 1), pltpu.roll(x, ML_DQK // 2, 1))
    return x * cos + swapped * sin


def _mlstm_kernel(q_ref, k_ref, v_ref, g_ref, gt_ref, gbr_ref, gbc_ref, cos_ref, sin_ref, o_ref,
                  c_ref, n_ref, m_ref, *, reverse):
    i = pl.program_id(1)
    d = 1 if reverse else 0

    @pl.when(i == 0)
    def _():
        c_ref[...] = jnp.zeros_like(c_ref)
        n_ref[...] = jnp.zeros_like(n_ref)
        m_ref[...] = jnp.zeros_like(m_ref)

    q = _rope(q_ref[0], cos_ref[...], sin_ref[...])
    k = _rope(k_ref[0] * (ML_DQK ** -0.5), cos_ref[...], sin_ref[...])
    g_col = g_ref[0] + gbr_ref[...]
    g_row = gt_ref[0] + gbc_ref[...]
    incl, _, incl_t = _tri_masks(ML_CHUNK, reverse)
    is_h0 = lax.broadcasted_iota(jnp.int32, (1, LANES), 1) < ML_DQK
    nc = TM // ML_CHUNK
    nh = ML_HEADS
    order = range(nc - 1, -1, -1) if reverse else range(nc)
    tokc = lambda ci: slice(ci * ML_CHUNK, (ci + 1) * ML_CHUNK)
    pair = lambda h: slice((h // 2) * LANES, (h // 2 + 1) * LANES)
    hmask = lambda h: is_h0 if h % 2 == 0 else jnp.logical_not(is_h0)
    stack = lambda fn: jnp.stack([fn(h, ci) for h in range(nh) for ci in range(nc)], axis=0)
    by_head = lambda a: a.reshape((nh, nc) + a.shape[1:])
    flat = lambda parts: jnp.stack(parts, axis=1).reshape((nh * nc,) + parts[0].shape[1:])
    qm = stack(lambda h, ci: jnp.where(hmask(h), q[tokc(ci), pair(h)], 0.0))
    km = stack(lambda h, ci: jnp.where(hmask(h), k[tokc(ci), pair(h)], 0.0))
    v = stack(lambda h, ci: v_ref[0, tokc(ci), h * ML_DV:(h + 1) * ML_DV])
    li_c = stack(lambda h, ci: g_col[tokc(ci), 8 * d + h:8 * d + h + 1])
    lf_c = jax.nn.log_sigmoid(stack(lambda h, ci: g_col[tokc(ci), 8 * d + 4 + h:8 * d + 5 + h]))
    li_r = stack(lambda h, ci: g_row[8 * d + h:8 * d + h + 1, tokc(ci)])
    lf_r = jax.nn.log_sigmoid(stack(lambda h, ci: g_row[8 * d + 4 + h:8 * d + 5 + h, tokc(ci)]))
    cum_c = jnp.sum(jnp.where(incl, lf_r, 0.0), axis=-1, keepdims=True)
    cum_r = jnp.sum(jnp.where(incl_t, lf_c, 0.0), axis=-2, keepdims=True)
    log_d = jnp.where(incl, cum_c - cum_r + li_r, NEG)
    m_loc = jnp.max(log_d, axis=-1, keepdims=True)
    cum_last = jnp.sum(lf_r, axis=-1, keepdims=True)
    log_w = cum_last - cum_c + li_c
    m_w = jnp.max(log_w, axis=-2, keepdims=True)
    cl4, mw4 = by_head(cum_last), by_head(m_w)
    m = m_ref[...][:, :, 0:1]
    m_prev, m_new = [None] * nc, [None] * nc
    for ci in order:
        m_prev[ci] = m
        m = jnp.maximum(cl4[:, ci] + m, mw4[:, ci])
        m_new[ci] = m
    m_ref[...] = jnp.broadcast_to(m, m_ref.shape)
    m_prev, m_new = flat(m_prev), flat(m_new)
    m_inter = cum_c + m_prev
    m_t = jnp.maximum(m_loc, m_inter)
    sg = _bmm_nt(qm, km) * jnp.exp(log_d - m_t)
    w_inter = jnp.exp(m_inter - m_t)
    kw = km * jnp.exp(log_w - m_new)
    upd_c, upd_n = by_head(_bmm_tn(kw, v)), by_head(jnp.sum(kw, axis=-2, keepdims=True))
    decay = by_head(jnp.exp(cum_last + m_prev - m_new))
    c_mat, n_vec = c_ref[...], n_ref[...]
    c_prev, n_prev = [None] * nc, [None] * nc
    for ci in order:
        c_prev[ci], n_prev[ci] = c_mat, n_vec
        c_mat = decay[:, ci] * c_mat + upd_c[:, ci]
        n_vec = decay[:, ci] * n_vec + upd_n[:, ci]
    c_ref[...] = c_mat
    n_ref[...] = n_vec
    num = _bmm(sg, v) + w_inter * _bmm(qm, flat(c_prev))
    den = jnp.sum(sg, axis=-1, keepdims=True) + w_inter * jnp.sum(qm * flat(n_prev), axis=-1, keepdims=True)
    h_out = by_head(num / jnp.maximum(jnp.abs(den), jnp.exp(-m_t)))
    for h in range(nh):
        for ci in range(nc):
            o_ref[0, 0, tokc(ci), h * ML_DV:(h + 1) * ML_DV] = h_out[h, ci]


def mlstm_direction(proj, gates_t, gate_b, cos, sin, reverse):
    b, t, _ = proj.shape
    n_tiles = t // TM
    tile = functools.partial(_scan_tile, n_tiles=n_tiles, reverse=reverse)
    gb = gate_b.astype(F32).reshape(16)
    gb_row = jnp.zeros((1, LANES), F32).at[0, :16].set(gb)
    gb_col = gb.reshape(16, 1)
    return pl.pallas_call(
        functools.partial(_mlstm_kernel, reverse=reverse),
        grid=(b, n_tiles),
        in_specs=[pl.BlockSpec((1, TM, ML_QK_W), lambda bi, i: (bi, tile(i), E_MQ // ML_QK_W)),
                  pl.BlockSpec((1, TM, ML_QK_W), lambda bi, i: (bi, tile(i), E_MK // ML_QK_W)),
                  pl.BlockSpec((1, TM, ML_V_W), lambda bi, i: (bi, tile(i), E_MV // ML_V_W)),
                  pl.BlockSpec((1, TM, LANES), lambda bi, i: (bi, tile(i), E_MG // LANES)),
                  pl.BlockSpec((1, 16, TM), lambda bi, i: (bi, 0, tile(i))),
                  pl.BlockSpec((1, LANES), lambda bi, i: (0, 0)),
                  pl.BlockSpec((16, 1), lambda bi, i: (0, 0)),
                  pl.BlockSpec((TM, ML_QK_W), lambda bi, i: (tile(i), 0)),
                  pl.BlockSpec((TM, ML_QK_W), lambda bi, i: (tile(i), 0))],
        out_specs=pl.BlockSpec((1, 1, TM, ML_V_W), lambda bi, i: (0, bi, tile(i), 0)),
        out_shape=jax.ShapeDtypeStruct((1, b, t, ML_V_W), F32),
        scratch_shapes=[pltpu.VMEM((ML_HEADS, LANES, ML_DV), F32),
                        pltpu.VMEM((ML_HEADS, 1, LANES), F32),
                        pltpu.VMEM((ML_HEADS, 1, LANES), F32)],
        compiler_params=_cp(("parallel", "arbitrary")),
        name="mlstm_bwd" if reverse else "mlstm_fwd",
    )(proj, proj, proj, proj, gates_t, gb_row, gb_col, cos, sin)[0]


def rope_tables(t):
    s = t - CTX_LEN
    tok = jnp.arange(s)
    n_freq = ML_DQK // 4
    inv = ROPE_THETA ** (-jnp.arange(n_freq, dtype=F32) / n_freq)
    ang = jnp.concatenate([(tok // GRID_W).astype(F32)[:, None] * inv,
                           (tok % GRID_W).astype(F32)[:, None] * inv], axis=-1)
    cos, sin = jnp.cos(ang), jnp.sin(ang)
    cos_h = jnp.concatenate([cos, cos], axis=-1)
    sin_h = jnp.concatenate([-sin, sin], axis=-1)
    cos_f = jnp.concatenate([jnp.ones((CTX_LEN, ML_DQK), F32), cos_h], axis=0)
    sin_f = jnp.concatenate([jnp.zeros((CTX_LEN, ML_DQK), F32), sin_h], axis=0)
    return jnp.tile(cos_f, (1, ML_HEADS)), jnp.tile(sin_f, (1, ML_HEADS))


def _dnconv_kernel(x_ref, w_ref, o_ref):
    cb = pl.program_id(1)
    x = x_ref[0]
    t = x.shape[0]
    tok = lax.broadcasted_iota(jnp.int32, (t, 1), 0)
    seg_lo = jnp.where(tok < CTX_LEN, 0, CTX_LEN)
    seg_hi = jnp.where(tok < CTX_LEN, CTX_LEN, t)
    acc = jnp.zeros_like(x)
    for j in range(DN_CONV):
        o = j - DN_CONV // 2
        xs = x if o == 0 else pltpu.roll(x, (-o) % t, 0)
        ok = (tok + o >= seg_lo) & (tok + o < seg_hi)
        acc = acc + jnp.where(ok, xs, 0.0) * w_ref[j:j + 1, :]
    y = _silu(acc)
    nrm = lax.rsqrt(jnp.sum(y * y, axis=-1, keepdims=True) + EPS)
    q_heads = DN_QK_W // LANES
    o_ref[0] = jnp.where(cb < q_heads, (y * nrm) * (DN_DK ** -0.5), jnp.where(cb < 2 * q_heads, y * nrm, y))


def dn_conv(proj, conv_w):
    b, t, _ = proj.shape
    return pl.pallas_call(
        _dnconv_kernel,
        grid=(b, DN_CONV_CH // LANES),
        in_specs=[pl.BlockSpec((1, t, LANES), lambda i, c: (i, 0, c)),
                  pl.BlockSpec((DN_CONV, LANES), lambda i, c: (0, c))],
        out_specs=pl.BlockSpec((1, t, LANES), lambda i, c: (i, 0, c)),
        out_shape=jax.ShapeDtypeStruct((b, t, DN_CONV_CH), F32),
        compiler_params=_cp(("parallel", "arbitrary")),
        name="dn_conv",
    )(proj, conv_w.astype(F32).T)


def _gdn_kernel(q_ref, k_ref, v_ref, ab_ref, at_ref, par_ref, parc_ref, o_ref, s_ref, *, reverse):
    i = pl.program_id(1)
    d = 1 if reverse else 0

    @pl.when(i == 0)
    def _():
        s_ref[...] = jnp.zeros_like(s_ref)

    incl, strict, incl_t = _tri_masks(DN_CHUNK, reverse)
    r = lax.broadcasted_iota(jnp.int32, (DN_CHUNK, DN_CHUNK), 0)
    c = lax.broadcasted_iota(jnp.int32, (DN_CHUNK, DN_CHUNK), 1)
    eye = (r == c).astype(F32)
    ab = ab_ref[0]
    g_all = -jnp.exp(par_ref[0:1, :]) * jax.nn.softplus(ab + par_ref[1:2, :])
    beta_full = jax.nn.sigmoid(ab)
    gt_all = -jnp.exp(parc_ref[:, 0:1]) * jax.nn.softplus(at_ref[0] + parc_ref[:, 1:2])

    nc = TM // DN_CHUNK
    nh = DN_HEADS
    tokc = lambda ci: slice(ci * DN_CHUNK, (ci + 1) * DN_CHUNK)
    hl = lambda h: slice(h * LANES, (h + 1) * LANES)
    stack = lambda fn: jnp.stack([fn(h, ci) for h in range(nh) for ci in range(nc)], axis=0)
    q = stack(lambda h, ci: q_ref[0, tokc(ci), hl(h)])
    k = stack(lambda h, ci: k_ref[0, tokc(ci), hl(h)])
    v = stack(lambda h, ci: v_ref[0, tokc(ci), hl(h)])
    g_c = stack(lambda h, ci: g_all[tokc(ci), 8 * d + h:8 * d + h + 1])
    beta = stack(lambda h, ci: beta_full[tokc(ci), 16 + 8 * d + h:17 + 8 * d + h])
    g_r = stack(lambda h, ci: gt_all[8 * d + h:8 * d + h + 1, tokc(ci)])
    gc_c = jnp.sum(jnp.where(incl, g_r, 0.0), axis=-1, keepdims=True)
    gc_r = jnp.sum(jnp.where(incl_t, g_c, 0.0), axis=-2, keepdims=True)
    decay = jnp.exp(jnp.where(incl, gc_c - gc_r, NEG))
    kb = k * beta
    a_mat = jnp.where(strict, _bmm_nt(kb, k) * decay, 0.0)
    pw = -a_mat
    t_inv = eye + pw
    for _ in range(5):
        pw = _bmm(pw, pw)
        t_inv = t_inv + _bmm(t_inv, pw)
    egc = jnp.exp(gc_c)
    uw = _bmm(t_inv, jnp.concatenate([v * beta, kb * egc], axis=-1))
    attn = _bmm_nt(q, k) * decay
    g_last = jnp.sum(g_c, axis=-2, keepdims=True)
    kg = k * jnp.exp(g_last - gc_c)
    ng = _bmm_tn(kg, uw)
    au = _bmm(attn, uw)
    lhs = jnp.concatenate([ng[:, :, DN_DV:], q * egc - au[:, :, DN_DV:]], axis=1)
    by_head = lambda a: a.reshape((nh, nc) + a.shape[1:])
    lhs, n_mat, o2, dec = by_head(lhs), by_head(ng[:, :, :DN_DV]), by_head(au[:, :, :DN_DV]), by_head(jnp.exp(g_last))
    s_mat = s_ref[...]
    for ci in (range(nc - 1, -1, -1) if reverse else range(nc)):
        prod = _bmm(lhs[:, ci], s_mat)
        o = prod[:, DN_DK:] + o2[:, ci]
        for h in range(nh):
            o_ref[0, 0, tokc(ci), hl(h)] = o[h]
        s_mat = s_mat * dec[:, ci] + n_mat[:, ci] - prod[:, :DN_DK]
    s_ref[...] = s_mat


def gdn_direction(qkv, proj, gates_t, a_log, dt_bias, reverse):
    b, t, _ = qkv.shape
    n_tiles = t // TM
    tile = functools.partial(_scan_tile, n_tiles=n_tiles, reverse=reverse)
    par = jnp.zeros((2, LANES), F32)
    par = par.at[0, :16].set(a_log.astype(F32).reshape(16)).at[1, :16].set(dt_bias.astype(F32).reshape(16))
    parc = par[:, :16].T
    return pl.pallas_call(
        functools.partial(_gdn_kernel, reverse=reverse),
        grid=(b, n_tiles),
        in_specs=[pl.BlockSpec((1, TM, DN_QK_W), lambda bi, i: (bi, tile(i), 0)),
                  pl.BlockSpec((1, TM, DN_QK_W), lambda bi, i: (bi, tile(i), 1)),
                  pl.BlockSpec((1, TM, DN_V_W), lambda bi, i: (bi, tile(i), 2)),
                  pl.BlockSpec((1, TM, LANES), lambda bi, i: (bi, tile(i), O_A // LANES)),
                  pl.BlockSpec((1, 16, TM), lambda bi, i: (bi, 0, tile(i))),
                  pl.BlockSpec((2, LANES), lambda bi, i: (0, 0)),
                  pl.BlockSpec((16, 2), lambda bi, i: (0, 0))],
        out_specs=pl.BlockSpec((1, 1, TM, DN_V_W), lambda bi, i: (0, bi, tile(i), 0)),
        out_shape=jax.ShapeDtypeStruct((1, b, t, DN_V_W), F32),
        scratch_shapes=[pltpu.VMEM((DN_HEADS, DN_DK, DN_DV), F32)],
        compiler_params=_cp(("parallel", "arbitrary")),
        name="gdn_bwd" if reverse else "gdn_fwd",
    )(qkv, qkv, qkv, proj, gates_t, par, parc)[0]


def _headwise_rms(x, gain, width):
    parts = []
    for h0 in range(0, x.shape[-1], width):
        xh = x[:, h0:h0 + width]
        parts.append((xh * lax.rsqrt(jnp.mean(xh * xh, axis=-1, keepdims=True) + EPS)) * gain)
    return parts


def _outproj_kernel(*refs, even):
    if even:
        (a_ref, hf_ref, hb_ref, og_ref, gain_ref, x_ref, mod_ref, g2_ref, w_ref, rw_ref, rb_ref,
         xo_ref, hm_ref, tv_ref, ti_ref) = refs
        hs = _headwise_rms(hf_ref[0] + hb_ref[0], gain_ref[...], ML_DV)
        og = og_ref[0]
        mix = [a_ref[0]] + [hs[h] * jax.nn.sigmoid(og[:, h * ML_DV:(h + 1) * ML_DV]) for h in range(ML_HEADS)]
    else:
        (of_ref, ob_ref, z_ref, gain_ref, x_ref, mod_ref, g2_ref, w_ref, rw_ref, rb_ref,
         xo_ref, hm_ref, tv_ref, ti_ref) = refs
        hs = _headwise_rms(of_ref[0] + ob_ref[0], gain_ref[...], DN_DV)
        z = z_ref[0]
        mix = [hs[h] * _silu(z[:, h * DN_DV:(h + 1) * DN_DV]) for h in range(DN_HEADS)]
    mix = jnp.concatenate(mix, axis=-1)
    x_new = x_ref[0] + mod_ref[2:3, :] * _mm(mix, w_ref[...])
    xo_ref[0] = x_new
    hm = _norm_mod(x_new, g2_ref[...], mod_ref[3:4, :], mod_ref[4:5, :])
    hm_ref[0] = hm.astype(hm_ref.dtype)
    rw = rw_ref[...]
    hm_hi, rw_hi = hm.astype(MXU_DT), rw.astype(MXU_DT)
    hm_lo, rw_lo = hm - hm_hi.astype(F32), rw - rw_hi.astype(F32)
    logits = (_mm(hm_hi, rw_hi) + (_mm(hm_hi, rw_lo) + _mm(hm_lo, rw_hi))) + rb_ref[...]
    lane = lax.broadcasted_iota(jnp.int32, logits.shape, 1)
    vals = jnp.zeros_like(logits)
    idxs = jnp.zeros(logits.shape, jnp.int32)
    work = jnp.where(lane < N_EXPERTS, logits, -jnp.inf)
    for kk in range(TOP_K):
        mx = jnp.max(work, axis=-1, keepdims=True)
        am = jnp.min(jnp.where(work == mx, lane, LANES), axis=-1, keepdims=True)
        vals = jnp.where(lane == kk, mx, vals)
        idxs = jnp.where(lane == kk, am, idxs)
        work = jnp.where(lane == am, -jnp.inf, work)
    ex = jnp.where(lane < TOP_K, jnp.exp(vals - jnp.max(jnp.where(lane < TOP_K, vals, -jnp.inf), axis=-1, keepdims=True)), 0.0)
    tv_ref[0] = ex / jnp.sum(ex, axis=-1, keepdims=True)
    ti_ref[0] = idxs


def out_projection(even, mixer_in, gain, x, mod, g2, w_out, router_w, router_b):
    b, t, d = x.shape
    tok = lambda wdt, blk: pl.BlockSpec((1, TM, wdt), lambda i, j: (i, j, blk))
    if even:
        a, hf, hb, proj = mixer_in
        ins = [a, hf, hb, proj]
        specs = [tok(NA_W, 0), tok(ML_V_W, 0), tok(ML_V_W, 0), tok(ML_V_W, E_MO // ML_V_W)]
    else:
        of, ob, proj = mixer_in
        ins = [of, ob, proj]
        specs = [tok(DN_V_W, 0), tok(DN_V_W, 0), tok(DN_V_W, O_Z // DN_V_W)]
    rw = jnp.zeros((d, LANES), F32).at[:, :N_EXPERTS].set(router_w.astype(F32))
    rb = jnp.zeros((1, LANES), F32).at[0, :N_EXPERTS].set(router_b.astype(F32))
    const = lambda shape: pl.BlockSpec(shape, lambda i, j: (0,) * len(shape))
    ins += [gain.astype(F32).reshape(1, LANES), x, mod, g2.reshape(1, d), w_out.astype(MXU_DT), rw, rb]
    specs += [const((1, LANES)), tok(d, 0),
              pl.BlockSpec((None, None, 6, d), lambda i, j: (i, jnp.minimum(j, 1), 0, 0)),
              const((1, d)), const(w_out.shape), const((d, LANES)), const((1, LANES))]
    return pl.pallas_call(
        functools.partial(_outproj_kernel, even=even),
        grid=(b, t // TM),
        in_specs=specs,
        out_specs=[tok(d, 0), tok(d, 0), tok(LANES, 0), tok(LANES, 0)],
        out_shape=[jax.ShapeDtypeStruct((b, t, d), F32), jax.ShapeDtypeStruct((b, t, d), MXU_DT),
                   jax.ShapeDtypeStruct((b, t, LANES), F32), jax.ShapeDtypeStruct((b, t, LANES), jnp.int32)],
        compiler_params=_cp(("parallel", "arbitrary")),
        name="out_projection_even" if even else "out_projection_odd",
    )(*ins)


MOE_GRP = 2 * LANES


def _swiglu_perm():
    p = np.zeros((MOE_GRP, MOE_GRP), np.float32)
    j = np.arange(LANES)
    p[2 * j, j] = 1.0
    p[2 * j + 1, LANES + j] = 1.0
    return p


def _swiglu_col_order(n):
    g = np.arange(n // MOE_GRP)[:, None] * MOE_GRP
    j = np.arange(LANES)[None, :]
    return np.concatenate([g + 2 * j, g + 2 * j + 1], axis=1).reshape(-1)


def _experts_kernel(be_ref, nb_ref, x_ref, w1_ref, b1_ref, w2_ref, b2_ref, perm_ref, o_ref, w1s_ref, w2s_ref):
    i = pl.program_id(0)
    n_grp = w1s_ref.shape[1] // MOE_GRP

    @pl.when((i == 0) | (be_ref[i] != be_ref[jnp.maximum(i - 1, 0)]))
    def _():
        for g in range(n_grp):
            cols = slice(g * MOE_GRP, (g + 1) * MOE_GRP)
            w1s_ref[:, cols] = jnp.dot(w1_ref[0, :, cols].astype(MXU_DT), perm_ref[...],
                                       preferred_element_type=F32).astype(MXU_DT)
        w2s_ref[...] = w2_ref[0].astype(MXU_DT)

    @pl.when(i < nb_ref[0])
    def _():
        x = x_ref[...]
        acts = []
        for g in range(n_grp):
            cols = slice(g * MOE_GRP, (g + 1) * MOE_GRP)
            hid = jnp.dot(x, w1s_ref[:, cols], preferred_element_type=F32) + b1_ref[0, :, cols]
            glu = jnp.minimum(hid[:, :LANES], SWIGLU_LIMIT)
            lin = jnp.clip(hid[:, LANES:], -SWIGLU_LIMIT, SWIGLU_LIMIT)
            acts.append((glu * jax.nn.sigmoid(SWIGLU_ALPHA * glu) * (lin + 1.0)).astype(MXU_DT))
        act = jnp.concatenate(acts, axis=-1)
        o_ref[...] = (jnp.dot(act, w2s_ref[...], preferred_element_type=F32) + b2_ref[0]).astype(o_ref.dtype)

    @pl.when(i >= nb_ref[0])
    def _():
        o_ref[...] = jnp.zeros_like(o_ref)


def experts_ffn(xs, block_exp, n_used, layer, w1, b1p, w2, b2):
    n_slots, d = xs.shape
    n_blocks = n_slots // MOE_BM
    dh2 = w1.shape[-1]
    wspec = lambda shape: pl.BlockSpec((None, 1) + shape, lambda i, be, nb: (layer, be[i], 0, 0))
    bspec = lambda shape: pl.BlockSpec((1,) + shape, lambda i, be, nb: (be[i], 0, 0))
    return pl.pallas_call(
        _experts_kernel,
        grid_spec=pltpu.PrefetchScalarGridSpec(
            num_scalar_prefetch=2,
            grid=(n_blocks,),
            in_specs=[pl.BlockSpec((MOE_BM, d), lambda i, be, nb: (jnp.minimum(i, nb[0] - 1), 0)),
                      wspec((d, dh2)), bspec((1, dh2)), wspec((dh2 // 2, d)), bspec((1, d)),
                      pl.BlockSpec((MOE_GRP, MOE_GRP), lambda i, be, nb: (0, 0))],
            out_specs=pl.BlockSpec((MOE_BM, d), lambda i, be, nb: (i, 0)),
            scratch_shapes=[pltpu.VMEM((d, dh2), MXU_DT), pltpu.VMEM((dh2 // 2, d), MXU_DT)],
        ),
        out_shape=jax.ShapeDtypeStruct((n_slots, d), MXU_DT),
        compiler_params=_cp(("arbitrary",)),
        name="experts_ffn",
    )(block_exp, n_used, xs, w1, b1p, w2, b2, jnp.asarray(_swiglu_perm(), MXU_DT))


def _combine_kernel(x_ref, y_ref, gate_ref, mod_ref, o_ref):
    gate = gate_ref[0]
    f = y_ref[0, 0].astype(F32) * gate[:, 0:1]
    for kk in range(1, TOP_K):
        f = f + y_ref[0, kk].astype(F32) * gate[:, kk:kk + 1]
    o_ref[0] = x_ref[0] + mod_ref[5:6, :] * f


def moe_combine(x, yg, gate, mod):
    b, t, d = x.shape
    return pl.pallas_call(
        _combine_kernel,
        grid=(b, t // TM),
        in_specs=[pl.BlockSpec((1, TM, d), lambda i, j: (i, j, 0)),
                  pl.BlockSpec((1, TOP_K, TM, d), lambda i, j: (i, 0, j, 0)),
                  pl.BlockSpec((1, TM, LANES), lambda i, j: (i, j, 0)),
                  pl.BlockSpec((None, None, 6, d), lambda i, j: (i, jnp.minimum(j, 1), 0, 0))],
        out_specs=pl.BlockSpec((1, TM, d), lambda i, j: (i, j, 0)),
        out_shape=jax.ShapeDtypeStruct((b, t, d), F32),
        compiler_params=_cp(("parallel", "arbitrary")),
        name="moe_combine",
    )(x, yg, gate, mod)


def moe_layer(x_new, hm, gate, top_idx, mod, layer, w1, b1, w2, b2):
    b, t, d = x_new.shape
    n_tok = b * t
    n_asg = n_tok * TOP_K
    flat_exp = top_idx[:, :, :TOP_K].transpose(0, 2, 1).reshape(n_asg)
    tok_of_asg = (jnp.arange(b, dtype=jnp.int32)[:, None, None] * t + jnp.zeros((1, TOP_K, 1), jnp.int32)
                  + jnp.arange(t, dtype=jnp.int32)[None, None, :]).reshape(n_asg)
    cb = 256
    onehot = (flat_exp[:, None] == jnp.arange(N_EXPERTS, dtype=jnp.int32)[None, :]).astype(F32)
    onehot = onehot.reshape(n_asg // cb, cb, N_EXPERTS)
    within = jnp.einsum('ij,bjk->bik', jnp.tril(jnp.ones((cb, cb), F32)), onehot)
    blk_tot = within[:, -1, :]
    blk_off = jnp.cumsum(blk_tot, axis=0) - blk_tot
    counts = (blk_off[-1] + blk_tot[-1]).astype(jnp.int32)
    rank = jnp.sum((within - onehot + blk_off[:, None, :]) * onehot, axis=-1).reshape(n_asg).astype(jnp.int32)
    padded = (counts + MOE_BM - 1) // MOE_BM * MOE_BM
    pad_end = jnp.cumsum(padded)
    dest = (pad_end - padded)[flat_exp] + rank
    n_blocks = -(-n_asg // MOE_BM) + N_EXPERTS
    n_slots = n_blocks * MOE_BM
    slot_tok = (jnp.arange(n_slots, dtype=jnp.int32) % n_tok).at[dest].set(tok_of_asg, unique_indices=True,
                                                                           mode='promise_in_bounds')
    block_start = jnp.arange(n_blocks, dtype=jnp.int32) * MOE_BM
    block_exp = jnp.minimum(jnp.sum((pad_end[None, :] <= block_start[:, None]).astype(jnp.int32), axis=1),
                            N_EXPERTS - 1)
    n_used = (pad_end[-1] // MOE_BM).astype(jnp.int32).reshape(1)
    xs = hm.reshape(n_tok, d)[slot_tok]
    b1p = b1.astype(F32)[:, _swiglu_col_order(b1.shape[-1])][:, None, :]
    y = experts_ffn(xs, block_exp, n_used, layer, w1, b1p, w2, b2[:, None, :].astype(F32))
    yg = y[dest].reshape(b, TOP_K, t, d)
    return moe_combine(x_new, yg, gate, mod)


def _pad_cols(w, n):
    return jnp.pad(w, ((0, 0), (0, n - w.shape[1]))).astype(MXU_DT)


N_STREAMS = 2


def kernel(x, c, ctx, c_ctx, ada_w, ada_b, norm1_g, norm2_g, ev_w_in, ev_w_out, na_q_gain, na_k_gain, na_rpb,
           ml_gate_b, ml_out_gain, od_w_in, od_w_out, dn_conv_w, dn_a_log, dn_dt_bias, dn_out_gain, router_w,
           router_b, exp_w1, exp_b1, exp_w2, exp_b2):
    b = x.shape[0]
    assert b % N_STREAMS == 0
    g = b // N_STREAMS
    outs = [_trunk(x[i:i + g], c[i:i + g], ctx[i:i + g], c_ctx, ada_w, ada_b, norm1_g, norm2_g, ev_w_in, ev_w_out,
                   na_q_gain, na_k_gain, na_rpb, ml_gate_b, ml_out_gain, od_w_in, od_w_out, dn_conv_w, dn_a_log,
                   dn_dt_bias, dn_out_gain, router_w, router_b, exp_w1, exp_b1, exp_w2, exp_b2)
            for i in range(0, b, g)]
    return jnp.concatenate(outs, axis=0)


def _trunk(x, c, ctx, c_ctx, ada_w, ada_b, norm1_g, norm2_g, ev_w_in, ev_w_out, na_q_gain, na_k_gain, na_rpb,
           ml_gate_b, ml_out_gain, od_w_in, od_w_out, dn_conv_w, dn_a_log, dn_dt_bias, dn_out_gain, router_w,
           router_b, exp_w1, exp_b1, exp_w2, exp_b2):
    b, s, d = x.shape
    depth = ada_w.shape[0]
    t = CTX_LEN + s
    xs = jnp.concatenate([ctx, x], axis=1)
    cond = jnp.zeros((16, d), F32).at[:b].set(c).at[b].set(c_ctx)
    mods = adaln_all(cond, ada_w, ada_b)
    mod_lat = mods[:, :b].reshape(depth, b, 1, 6, d)
    mod_ctx = jnp.broadcast_to(mods[:, b].reshape(depth, 1, 1, 6, d), (depth, b, 1, 6, d))
    mod_all = jnp.concatenate([mod_ctx, mod_lat], axis=2)
    cos, sin = rope_tables(t)
    for layer in range(depth):
        mod = mod_all[layer]
        if layer % 2 == 0:
            e = layer // 2
            proj = in_projection(xs, mod, norm1_g[layer], _pad_cols(ev_w_in[e], EVEN_PAD), 640)
            a = neighbourhood_attention(proj, na_q_gain[e], na_k_gain[e], na_bias_tables(na_rpb[e]))
            gates_t = proj[:, :, E_MG:E_MG + 16].transpose(0, 2, 1)
            hf = mlstm_direction(proj, gates_t, ml_gate_b[e], cos, sin, False)
            hb = mlstm_direction(proj, gates_t, ml_gate_b[e], cos, sin, True)
            mixer_in, gain, w_out = (a, hf, hb, proj), ml_out_gain[e], ev_w_out[e]
        else:
            o = layer // 2
            proj = in_projection(xs, mod, norm1_g[layer], _pad_cols(od_w_in[o], ODD_PAD), 1056)
            qkv = dn_conv(proj, dn_conv_w[o])
            gates_t = proj[:, :, O_A:O_A + 16].transpose(0, 2, 1)
            of = gdn_direction(qkv, proj, gates_t, dn_a_log[o], dn_dt_bias[o], False)
            ob = gdn_direction(qkv, proj, gates_t, dn_a_log[o], dn_dt_bias[o], True)
            mixer_in, gain, w_out = (of, ob, proj), dn_out_gain[o], od_w_out[o]
        x_new, hm, gate, top_idx = out_projection(layer % 2 == 0, mixer_in, gain, xs, mod, norm2_g[layer], w_out,
                                                  router_w[layer], router_b[layer])
        xs = moe_layer(x_new, hm, gate, top_idx, mod, layer, exp_w1, exp_b1[layer], exp_w2, exp_b2[layer])
    return xs[:, CTX_LEN:]
```

```python
import functools

import numpy as np
import jax
import jax.numpy as jnp
from jax import lax
from jax.experimental import pallas as pl
from jax.experimental.pallas import tpu as pltpu

F32 = jnp.float32
MXU_DT = jnp.bfloat16
EPS = 1e-6
NEG = -1e30

D_MODEL = 1024
GRID_W = 64
CTX_LEN = 256
TM = 256
LANES = 128

NA_HEADS, NA_DH = 8, 64
NA_W = NA_HEADS * NA_DH
NA_WIN_R, NA_WIN_C = 8, 16
NA_RB = 4
NA_QB = NA_RB * GRID_W
NA_KROWS = NA_RB + NA_WIN_R - 1
NA_KU = NA_KROWS * GRID_W

ML_HEADS, ML_DQK, ML_DV, ML_CHUNK = 4, 64, 128, 64
ML_QK_W, ML_V_W = ML_HEADS * ML_DQK, ML_HEADS * ML_DV
ROPE_THETA = 10000.0
EVEN_IN = 3 * NA_W + 2 * ML_QK_W + 2 * ML_V_W + 4 * ML_HEADS
EVEN_PAD = 3200
E_Q, E_K, E_V = 0, NA_W, 2 * NA_W
E_MQ = 3 * NA_W
E_MK = E_MQ + ML_QK_W
E_MV = E_MK + ML_QK_W
E_MO = E_MV + ML_V_W
E_MG = E_MO + ML_V_W

DN_HEADS, DN_DK, DN_DV, DN_CHUNK, DN_CONV = 8, 128, 128, 64, 5
DN_QK_W, DN_V_W = DN_HEADS * DN_DK, DN_HEADS * DN_DV
DN_CONV_CH = 2 * DN_QK_W + DN_V_W
ODD_IN = DN_CONV_CH + DN_V_W + 4 * DN_HEADS
ODD_PAD = 4224
O_Z = DN_CONV_CH
O_A = O_Z + DN_V_W

N_EXPERTS, TOP_K, D_EXPERT = 32, 4, 1024
SWIGLU_LIMIT, SWIGLU_ALPHA = 7.0, 1.702
MOE_BM = 512

VMEM_LIMIT = 56 * 1024 * 1024


def _cp(sem):
    return pltpu.CompilerParams(dimension_semantics=sem, vmem_limit_bytes=VMEM_LIMIT)


def _mm(a, b):
    return jnp.dot(a.astype(MXU_DT), b.astype(MXU_DT), preferred_element_type=F32)


def _mm_nt(a, b):
    return lax.dot_general(a.astype(MXU_DT), b.astype(MXU_DT), (((1,), (1,)), ((), ())),
                           preferred_element_type=F32)


def _mm_tn(a, b):
    return lax.dot_general(a.astype(MXU_DT), b.astype(MXU_DT), (((0,), (0,)), ((), ())),
                           preferred_element_type=F32)


def _bmm(a, b):
    return jnp.einsum('bij,bjk->bik', a.astype(MXU_DT), b.astype(MXU_DT), preferred_element_type=F32)


def _bmm_nt(a, b):
    return jnp.einsum('bik,bjk->bij', a.astype(MXU_DT), b.astype(MXU_DT), preferred_element_type=F32)


def _bmm_tn(a, b):
    return jnp.einsum('bki,bkj->bij', a.astype(MXU_DT), b.astype(MXU_DT), preferred_element_type=F32)


def _silu(x):
    return x * jax.nn.sigmoid(x)


def _adaln_kernel(c_ref, w_ref, b_ref, o_ref):
    o_ref[0] = _mm(_silu(c_ref[...]), w_ref[0]) + b_ref[0]


def adaln_all(cond, ada_w, ada_b):
    depth, d, n = ada_w.shape
    tn = 1536
    return pl.pallas_call(
        _adaln_kernel,
        grid=(depth, n // tn),
        in_specs=[pl.BlockSpec(cond.shape, lambda l, j: (0, 0)),
                  pl.BlockSpec((1, d, tn), lambda l, j: (l, 0, j)),
                  pl.BlockSpec((1, 1, tn), lambda l, j: (l, 0, j))],
        out_specs=pl.BlockSpec((1, cond.shape[0], tn), lambda l, j: (l, 0, j)),
        out_shape=jax.ShapeDtypeStruct((depth, cond.shape[0], n), F32),
        compiler_params=_cp(("arbitrary", "arbitrary")),
        name="adaln",
    )(cond, ada_w, ada_b.reshape(depth, 1, n))


def _norm_mod(x, gain, shift, scale):
    y = x * lax.rsqrt(jnp.mean(x * x, axis=-1, keepdims=True) + EPS)
    return (y * gain) * (1.0 + scale) + shift


def _inproj_kernel(x_ref, mod_ref, g_ref, w_ref, o_ref, *, n_chunk):
    h = _norm_mod(x_ref[0], g_ref[...], mod_ref[0:1, :], mod_ref[1:2, :]).astype(MXU_DT)
    for n0 in range(0, o_ref.shape[-1], n_chunk):
        o_ref[0, :, n0:n0 + n_chunk] = jnp.dot(h, w_ref[:, n0:n0 + n_chunk], preferred_element_type=F32)


def in_projection(x, mod, gain, w, n_chunk):
    b, t, d = x.shape
    n = w.shape[1]
    return pl.pallas_call(
        functools.partial(_inproj_kernel, n_chunk=n_chunk),
        grid=(b, t // TM),
        in_specs=[pl.BlockSpec((1, TM, d), lambda i, j: (i, j, 0)),
                  pl.BlockSpec((None, None, 6, d), lambda i, j: (i, jnp.minimum(j, 1), 0, 0)),
                  pl.BlockSpec((1, d), lambda i, j: (0, 0)),
                  pl.BlockSpec((d, n), lambda i, j: (0, 0))],
        out_specs=pl.BlockSpec((1, TM, n), lambda i, j: (i, j, 0)),
        out_shape=jax.ShapeDtypeStruct((b, t, n), F32),
        compiler_params=_cp(("parallel", "arbitrary")),
        name="in_projection",
    )(x, mod, gain.reshape(1, d), w)


def _na_block_start(j):
    return jnp.clip(NA_RB * (j - 1) - NA_WIN_R // 2, 0, GRID_W - NA_KROWS)


def na_bias_tables(rpb):
    rows = GRID_W
    n_dr = 2 * NA_WIN_R - 1
    cq = np.arange(GRID_W)[:, None]
    ck = np.arange(GRID_W)[None, :]
    c0 = np.clip(cq - NA_WIN_C // 2, 0, GRID_W - NA_WIN_C)
    ok_c = (ck >= c0) & (ck < c0 + NA_WIN_C)
    ic = np.clip(ck - cq + NA_WIN_C - 1, 0, 2 * NA_WIN_C - 2)
    tiles = jnp.where(ok_c[None, None], rpb.astype(F32)[:, :, ic], NEG)
    tiles = jnp.concatenate([tiles, jnp.full_like(tiles[:, :1], NEG)], axis=1)
    idx = np.full((4, NA_RB, NA_KROWS), n_dr, np.int32)
    for var, rb in enumerate((0, 1, rows // NA_RB - 1)):
        start = int(np.clip(NA_RB * rb - NA_WIN_R // 2, 0, rows - NA_KROWS))
        for qi in range(NA_RB):
            rq = NA_RB * rb + qi
            r0 = int(np.clip(rq - NA_WIN_R // 2, 0, rows - NA_WIN_R))
            for kj in range(NA_KROWS):
                rk = start + kj
                if r0 <= rk < r0 + NA_WIN_R:
                    idx[var, qi, kj] = rk - rq + NA_WIN_R - 1
    full = tiles[:, idx]
    return full.transpose(0, 1, 2, 4, 3, 5).reshape(rpb.shape[0], 4, NA_QB, NA_KU)


def _na_kernel(q_ref, k_ref, v_ref, qg_ref, kg_ref, bias_ref, o_ref, kn_ref):
    j = pl.program_id(2)
    is_h0 = lax.broadcasted_iota(jnp.int32, (1, LANES), 1) < NA_DH

    def headnorm(x, gain):
        x2 = x * x
        s0 = jnp.sum(jnp.where(is_h0, x2, 0.0), axis=-1, keepdims=True)
        s1 = jnp.sum(jnp.where(is_h0, 0.0, x2), axis=-1, keepdims=True)
        ms = jnp.where(is_h0, s0, s1) * (1.0 / NA_DH)
        return (x * lax.rsqrt(ms + EPS)) * gain

    @pl.when(j == 0)
    def _():
        kn_ref[...] = headnorm(k_ref[0], kg_ref[...]).astype(MXU_DT)

    qn = headnorm(q_ref[0], qg_ref[...]) * (NA_DH ** -0.5)
    off = pl.multiple_of(CTX_LEN + _na_block_start(j) * GRID_W, GRID_W)
    k_loc = kn_ref[pl.ds(off, NA_KU), :]
    k_ctx = kn_ref[0:CTX_LEN, :]
    v_loc = v_ref[0, pl.ds(off, NA_KU), :].astype(MXU_DT)
    v_ctx = v_ref[0, 0:CTX_LEN, :].astype(MXU_DT)
    outs = []
    for h in range(2):
        qm = jnp.where(is_h0 if h == 0 else jnp.logical_not(is_h0), qn, 0.0)
        s_loc = _mm_nt(qm, k_loc) + bias_ref[h, 0]
        s_ctx = _mm_nt(qm, k_ctx)
        m = jnp.maximum(jnp.max(s_loc, axis=-1, keepdims=True), jnp.max(s_ctx, axis=-1, keepdims=True))
        p_loc = jnp.exp(s_loc - m)
        p_ctx = jnp.exp(s_ctx - m)
        den = jnp.sum(p_loc, axis=-1, keepdims=True) + jnp.sum(p_ctx, axis=-1, keepdims=True)
        outs.append((_mm(p_loc, v_loc) + _mm(p_ctx, v_ctx)) / den)
    o_ref[0] = jnp.where(is_h0, outs[0], outs[1])


def neighbourhood_attention(proj, q_gain, k_gain, bias):
    b, t, _ = proj.shape
    nq = t // NA_QB
    n_rb = GRID_W // NA_RB

    def bias_idx(i, hp, j):
        var = jnp.where(j == 0, 3, jnp.where(j == 1, 0, jnp.where(j == n_rb, 2, 1)))
        return (hp, var, 0, 0)

    tile2 = lambda g: jnp.tile(g.astype(F32), 2).reshape(1, LANES)
    return pl.pallas_call(
        _na_kernel,
        grid=(b, NA_HEADS // 2, nq),
        in_specs=[pl.BlockSpec((1, NA_QB, LANES), lambda i, hp, j: (i, j, E_Q // LANES + hp)),
                  pl.BlockSpec((1, t, LANES), lambda i, hp, j: (i, 0, E_K // LANES + hp)),
                  pl.BlockSpec((1, t, LANES), lambda i, hp, j: (i, 0, E_V // LANES + hp)),
                  pl.BlockSpec((1, LANES), lambda i, hp, j: (0, 0)),
                  pl.BlockSpec((1, LANES), lambda i, hp, j: (0, 0)),
                  pl.BlockSpec((2, 1, NA_QB, NA_KU), bias_idx)],
        out_specs=pl.BlockSpec((1, NA_QB, LANES), lambda i, hp, j: (i, j, hp)),
        out_shape=jax.ShapeDtypeStruct((b, t, NA_W), F32),
        scratch_shapes=[pltpu.VMEM((t, LANES), MXU_DT)],
        compiler_params=_cp(("parallel", "arbitrary", "arbitrary")),
        name="neighbourhood_attention",
    )(proj, proj, proj, tile2(q_gain), tile2(k_gain), bias)


def _scan_tile(i, n_tiles, reverse):
    if not reverse:
        return i
    return jnp.where(i == 0, 0, n_tiles - i)


def _tri_masks(n, reverse):
    r = lax.broadcasted_iota(jnp.int32, (n, n), 0)
    c = lax.broadcasted_iota(jnp.int32, (n, n), 1)
    if reverse:
        return c >= r, c > r, r >= c
    return c <= r, c < r, r <= c


def _cumsum_both(x_col, x_row, incl, incl_t):
    cum_col = jnp.sum(jnp.where(incl, x_row, 0.0), axis=1, keepdims=True)
    cum_row = jnp.sum(jnp.where(incl_t, x_col, 0.0), axis=0, keepdims=True)
    return cum_col, cum_row


def _rope(x, cos, sin):
    w = x.shape[-1]
    lane = lax.broadcasted_iota(jnp.int32, (1, w), 1)
    first = (lane % ML_DQK) < ML_DQK // 2
    swapped = jnp.where(first, pltpu.roll(x, w - ML_DQK // 2, 1), pltpu.roll(x, ML_DQK // 2, 1))
    return x * cos + swapped * sin


def _mlstm_kernel(q_ref, k_ref, v_ref, g_ref, gt_ref, gbr_ref, gbc_ref, cos_ref, sin_ref, o_ref,
                  c_ref, n_ref, m_ref, *, reverse):
    i = pl.program_id(1)
    d = 1 if reverse else 0

    @pl.when(i == 0)
    def _():
        c_ref[...] = jnp.zeros_like(c_ref)
        n_ref[...] = jnp.zeros_like(n_ref)
        m_ref[...] = jnp.zeros_like(m_ref)

    q = _rope(q_ref[0], cos_ref[...], sin_ref[...])
    k = _rope(k_ref[0] * (ML_DQK ** -0.5), cos_ref[...], sin_ref[...])
    g_col = g_ref[0] + gbr_ref[...]
    g_row = gt_ref[0] + gbc_ref[...]
    incl, _, incl_t = _tri_masks(ML_CHUNK, reverse)
    is_h0 = lax.broadcasted_iota(jnp.int32, (1, LANES), 1) < ML_DQK
    nc = TM // ML_CHUNK
    nh = ML_HEADS
    order = range(nc - 1, -1, -1) if reverse else range(nc)
    tokc = lambda ci: slice(ci * ML_CHUNK, (ci + 1) * ML_CHUNK)
    pair = lambda h: slice((h // 2) * LANES, (h // 2 + 1) * LANES)
    hmask = lambda h: is_h0 if h % 2 == 0 else jnp.logical_not(is_h0)
    stack = lambda fn: jnp.stack([fn(h, ci) for h in range(nh) for ci in range(nc)], axis=0)
    by_head = lambda a: a.reshape((nh, nc) + a.shape[1:])
    flat = lambda parts: jnp.stack(parts, axis=1).reshape((nh * nc,) + parts[0].shape[1:])
    qm = stack(lambda h, ci: jnp.where(hmask(h), q[tokc(ci), pair(h)], 0.0))
    km = stack(lambda h, ci: jnp.where(hmask(h), k[tokc(ci), pair(h)], 0.0))
    v = stack(lambda h, ci: v_ref[0, tokc(ci), h * ML_DV:(h + 1) * ML_DV])
    li_c = stack(lambda h, ci: g_col[tokc(ci), 8 * d + h:8 * d + h + 1])
    lf_c = jax.nn.log_sigmoid(stack(lambda h, ci: g_col[tokc(ci), 8 * d + 4 + h:8 * d + 5 + h]))
    li_r = stack(lambda h, ci: g_row[8 * d + h:8 * d + h + 1, tokc(ci)])
    lf_r = jax.nn.log_sigmoid(stack(lambda h, ci: g_row[8 * d + 4 + h:8 * d + 5 + h, tokc(ci)]))
    cum_c = jnp.sum(jnp.where(incl, lf_r, 0.0), axis=-1, keepdims=True)
    cum_r = jnp.sum(jnp.where(incl_t, lf_c, 0.0), axis=-2, keepdims=True)
    log_d = jnp.where(incl, cum_c - cum_r + li_r, NEG)
    m_loc = jnp.max(log_d, axis=-1, keepdims=True)
    cum_last = jnp.sum(lf_r, axis=-1, keepdims=True)
    log_w = cum_last - cum_c + li_c
    m_w = jnp.max(log_w, axis=-2, keepdims=True)
    cl4, mw4 = by_head(cum_last), by_head(m_w)
    m = m_ref[...][:, :, 0:1]
    m_prev, m_new = [None] * nc, [None] * nc
    for ci in order:
        m_prev[ci] = m
        m = jnp.maximum(cl4[:, ci] + m, mw4[:, ci])
        m_new[ci] = m
    m_ref[...] = jnp.broadcast_to(m, m_ref.shape)
    m_prev, m_new = flat(m_prev), flat(m_new)
    m_inter = cum_c + m_prev
    m_t = jnp.maximum(m_loc, m_inter)
    sg = _bmm_nt(qm, km) * jnp.exp(log_d - m_t)
    w_inter = jnp.exp(m_inter - m_t)
    kw = km * jnp.exp(log_w - m_new)
    upd_c, upd_n = by_head(_bmm_tn(kw, v)), by_head(jnp.sum(kw, axis=-2, keepdims=True))
    decay = by_head(jnp.exp(cum_last + m_prev - m_new))
    c_mat, n_vec = c_ref[...], n_ref[...]
    c_prev, n_prev = [None] * nc, [None] * nc
    for ci in order:
        c_prev[ci], n_prev[ci] = c_mat, n_vec
        c_mat = decay[:, ci] * c_mat + upd_c[:, ci]
        n_vec = decay[:, ci] * n_vec + upd_n[:, ci]
    c_ref[...] = c_mat
    n_ref[...] = n_vec
    num = _bmm(sg, v) + w_inter * _bmm(qm, flat(c_prev))
    den = jnp.sum(sg, axis=-1, keepdims=True) + w_inter * jnp.sum(qm * flat(n_prev), axis=-1, keepdims=True)
    h_out = by_head(num / jnp.maximum(jnp.abs(den), jnp.exp(-m_t)))
    for h in range(nh):
        for ci in range(nc):
            o_ref[0, 0, tokc(ci), h * ML_DV:(h + 1) * ML_DV] = h_out[h, ci]


def mlstm_direction(proj, gates_t, gate_b, cos, sin, reverse):
    b, t, _ = proj.shape
    n_tiles = t // TM
    tile = functools.partial(_scan_tile, n_tiles=n_tiles, reverse=reverse)
    gb = gate_b.astype(F32).reshape(16)
    gb_row = jnp.zeros((1, LANES), F32).at[0, :16].set(gb)
    gb_col = gb.reshape(16, 1)
    return pl.pallas_call(
        functools.partial(_mlstm_kernel, reverse=reverse),
        grid=(b, n_tiles),
        in_specs=[pl.BlockSpec((1, TM, ML_QK_W), lambda bi, i: (bi, tile(i), E_MQ // ML_QK_W)),
                  pl.BlockSpec((1, TM, ML_QK_W), lambda bi, i: (bi, tile(i), E_MK // ML_QK_W)),
                  pl.BlockSpec((1, TM, ML_V_W), lambda bi, i: (bi, tile(i), E_MV // ML_V_W)),
                  pl.BlockSpec((1, TM, LANES), lambda bi, i: (bi, tile(i), E_MG // LANES)),
                  pl.BlockSpec((1, 16, TM), lambda bi, i: (bi, 0, tile(i))),
                  pl.BlockSpec((1, LANES), lambda bi, i: (0, 0)),
                  pl.BlockSpec((16, 1), lambda bi, i: (0, 0)),
                  pl.BlockSpec((TM, ML_QK_W), lambda bi, i: (tile(i), 0)),
                  pl.BlockSpec((TM, ML_QK_W), lambda bi, i: (tile(i), 0))],
        out_specs=pl.BlockSpec((1, 1, TM, ML_V_W), lambda bi, i: (0, bi, tile(i), 0)),
        out_shape=jax.ShapeDtypeStruct((1, b, t, ML_V_W), F32),
        scratch_shapes=[pltpu.VMEM((ML_HEADS, LANES, ML_DV), F32),
                        pltpu.VMEM((ML_HEADS, 1, LANES), F32),
                        pltpu.VMEM((ML_HEADS, 1, LANES), F32)],
        compiler_params=_cp(("parallel", "arbitrary")),
        name="mlstm_bwd" if reverse else "mlstm_fwd",
    )(proj, proj, proj, proj, gates_t, gb_row, gb_col, cos, sin)[0]


def rope_tables(t):
    s = t - CTX_LEN
    tok = jnp.arange(s)
    n_freq = ML_DQK // 4
    inv = ROPE_THETA ** (-jnp.arange(n_freq, dtype=F32) / n_freq)
    ang = jnp.concatenate([(tok // GRID_W).astype(F32)[:, None] * inv,
                           (tok % GRID_W).astype(F32)[:, None] * inv], axis=-1)
    cos, sin = jnp.cos(ang), jnp.sin(ang)
    cos_h = jnp.concatenate([cos, cos], axis=-1)
    sin_h = jnp.concatenate([-sin, sin], axis=-1)
    cos_f = jnp.concatenate([jnp.ones((CTX_LEN, ML_DQK), F32), cos_h], axis=0)
    sin_f = jnp.concatenate([jnp.zeros((CTX_LEN, ML_DQK), F32), sin_h], axis=0)
    return jnp.tile(cos_f, (1, ML_HEADS)), jnp.tile(sin_f, (1, ML_HEADS))


def _dnconv_kernel(x_ref, w_ref, o_ref):
    cb = pl.program_id(1)
    x = x_ref[0]
    t = x.shape[0]
    tok = lax.broadcasted_iota(jnp.int32, (t, 1), 0)
    seg_lo = jnp.where(tok < CTX_LEN, 0, CTX_LEN)
    seg_hi = jnp.where(tok < CTX_LEN, CTX_LEN, t)
    acc = jnp.zeros_like(x)
    for j in range(DN_CONV):
        o = j - DN_CONV // 2
        xs = x if o == 0 else pltpu.roll(x, (-o) % t, 0)
        ok = (tok + o >= seg_lo) & (tok + o < seg_hi)
        acc = acc + jnp.where(ok, xs, 0.0) * w_ref[j:j + 1, :]
    y = _silu(acc)
    nrm = lax.rsqrt(jnp.sum(y * y, axis=-1, keepdims=True) + EPS)
    q_heads = DN_QK_W // LANES
    o_ref[0] = jnp.where(cb < q_heads, (y * nrm) * (DN_DK ** -0.5), jnp.where(cb < 2 * q_heads, y * nrm, y))


def dn_conv(proj, conv_w):
    b, t, _ = proj.shape
    return pl.pallas_call(
        _dnconv_kernel,
        grid=(b, DN_CONV_CH // LANES),
        in_specs=[pl.BlockSpec((1, t, LANES), lambda i, c: (i, 0, c)),
                  pl.BlockSpec((DN_CONV, LANES), lambda i, c: (0, c))],
        out_specs=pl.BlockSpec((1, t, LANES), lambda i, c: (i, 0, c)),
        out_shape=jax.ShapeDtypeStruct((b, t, DN_CONV_CH), F32),
        compiler_params=_cp(("parallel", "arbitrary")),
        name="dn_conv",
    )(proj, conv_w.astype(F32).T)


def _gdn_kernel(q_ref, k_ref, v_ref, ab_ref, at_ref, par_ref, parc_ref, o_ref, s_ref, *, reverse):
    i = pl.program_id(1)
    d = 1 if reverse else 0

    @pl.when(i == 0)
    def _():
        s_ref[...] = jnp.zeros_like(s_ref)

    incl, strict, incl_t = _tri_masks(DN_CHUNK, reverse)
    r = lax.broadcasted_iota(jnp.int32, (DN_CHUNK, DN_CHUNK), 0)
    c = lax.broadcasted_iota(jnp.int32, (DN_CHUNK, DN_CHUNK), 1)
    eye = (r == c).astype(F32)
    ab = ab_ref[0]
    g_all = -jnp.exp(par_ref[0:1, :]) * jax.nn.softplus(ab + par_ref[1:2, :])
    beta_full = jax.nn.sigmoid(ab)
    gt_all = -jnp.exp(parc_ref[:, 0:1]) * jax.nn.softplus(at_ref[0] + parc_ref[:, 1:2])

    nc = TM // DN_CHUNK
    nh = DN_HEADS
    tokc = lambda ci: slice(ci * DN_CHUNK, (ci + 1) * DN_CHUNK)
    hl = lambda h: slice(h * LANES, (h + 1) * LANES)
    stack = lambda fn: jnp.stack([fn(h, ci) for h in range(nh) for ci in range(nc)], axis=0)
    q = stack(lambda h, ci: q_ref[0, tokc(ci), hl(h)])
    k = stack(lambda h, ci: k_ref[0, tokc(ci), hl(h)])
    v = stack(lambda h, ci: v_ref[0, tokc(ci), hl(h)])
    g_c = stack(lambda h, ci: g_all[tokc(ci), 8 * d + h:8 * d + h + 1])
    beta = stack(lambda h, ci: beta_full[tokc(ci), 16 + 8 * d + h:17 + 8 * d + h])
    g_r = stack(lambda h, ci: gt_all[8 * d + h:8 * d + h + 1, tokc(ci)])
    gc_c = jnp.sum(jnp.where(incl, g_r, 0.0), axis=-1, keepdims=True)
    gc_r = jnp.sum(jnp.where(incl_t, g_c, 0.0), axis=-2, keepdims=True)
    decay = jnp.exp(jnp.where(incl, gc_c - gc_r, NEG))
    kb = k * beta
    a_mat = jnp.where(strict, _bmm_nt(kb, k) * decay, 0.0)
    pw = -a_mat
    t_inv = eye + pw
    for _ in range(5):
        pw = _bmm(pw, pw)
        t_inv = t_inv + _bmm(t_inv, pw)
    egc = jnp.exp(gc_c)
    uw = _bmm(t_inv, jnp.concatenate([v * beta, kb * egc], axis=-1))
    attn = _bmm_nt(q, k) * decay
    g_last = jnp.sum(g_c, axis=-2, keepdims=True)
    kg = k * jnp.exp(g_last - gc_c)
    ng = _bmm_tn(kg, uw)
    au = _bmm(attn, uw)
    lhs = jnp.concatenate([ng[:, :, DN_DV:], q * egc - au[:, :, DN_DV:]], axis=1)
    by_head = lambda a: a.reshape((nh, nc) + a.shape[1:])
    lhs, n_mat, o2, dec = by_head(lhs), by_head(ng[:, :, :DN_DV]), by_head(au[:, :, :DN_DV]), by_head(jnp.exp(g_last))
    s_mat = s_ref[...]
    for ci in (range(nc - 1, -1, -1) if reverse else range(nc)):
        prod = _bmm(lhs[:, ci], s_mat)
        o = prod[:, DN_DK:] + o2[:, ci]
        for h in range(nh):
            o_ref[0, 0, tokc(ci), hl(h)] = o[h]
        s_mat = s_mat * dec[:, ci] + n_mat[:, ci] - prod[:, :DN_DK]
    s_ref[...] = s_mat


def gdn_direction(qkv, proj, gates_t, a_log, dt_bias, reverse):
    b, t, _ = qkv.shape
    n_tiles = t // TM
    tile = functools.partial(_scan_tile, n_tiles=n_tiles, reverse=reverse)
    par = jnp.zeros((2, LANES), F32)
    par = par.at[0, :16].set(a_log.astype(F32).reshape(16)).at[1, :16].set(dt_bias.astype(F32).reshape(16))
    parc = par[:, :16].T
    return pl.pallas_call(
        functools.partial(_gdn_kernel, reverse=reverse),
        grid=(b, n_tiles),
        in_specs=[pl.BlockSpec((1, TM, DN_QK_W), lambda bi, i: (bi, tile(i), 0)),
                  pl.BlockSpec((1, TM, DN_QK_W), lambda bi, i: (bi, tile(i), 1)),
                  pl.BlockSpec((1, TM, DN_V_W), lambda bi, i: (bi, tile(i), 2)),
                  pl.BlockSpec((1, TM, LANES), lambda bi, i: (bi, tile(i), O_A // LANES)),
                  pl.BlockSpec((1, 16, TM), lambda bi, i: (bi, 0, tile(i))),
                  pl.BlockSpec((2, LANES), lambda bi, i: (0, 0)),
                  pl.BlockSpec((16, 2), lambda bi, i: (0, 0))],
        out_specs=pl.BlockSpec((1, 1, TM, DN_V_W), lambda bi, i: (0, bi, tile(i), 0)),
        out_shape=jax.ShapeDtypeStruct((1, b, t, DN_V_W), F32),
        scratch_shapes=[pltpu.VMEM((DN_HEADS, DN_DK, DN_DV), F32)],
        compiler_params=_cp(("parallel", "arbitrary")),
        name="gdn_bwd" if reverse else "gdn_fwd",
    )(qkv, qkv, qkv, proj, gates_t, par, parc)[0]


def _headwise_rms(x, gain, width):
    parts = []
    for h0 in range(0, x.shape[-1], width):
        xh = x[:, h0:h0 + width]
        parts.append((xh * lax.rsqrt(jnp.mean(xh * xh, axis=-1, keepdims=True) + EPS)) * gain)
    return parts


def _outproj_kernel(*refs, even):
    if even:
        (a_ref, hf_ref, hb_ref, og_ref, gain_ref, x_ref, mod_ref, g2_ref, w_ref, rw_ref, rb_ref,
         xo_ref, hm_ref, tv_ref, ti_ref) = refs
        hs = _headwise_rms(hf_ref[0] + hb_ref[0], gain_ref[...], ML_DV)
        og = og_ref[0]
        mix = [a_ref[0]] + [hs[h] * jax.nn.sigmoid(og[:, h * ML_DV:(h + 1) * ML_DV]) for h in range(ML_HEADS)]
    else:
        (of_ref, ob_ref, z_ref, gain_ref, x_ref, mod_ref, g2_ref, w_ref, rw_ref, rb_ref,
         xo_ref, hm_ref, tv_ref, ti_ref) = refs
        hs = _headwise_rms(of_ref[0] + ob_ref[0], gain_ref[...], DN_DV)
        z = z_ref[0]
        mix = [hs[h] * _silu(z[:, h * DN_DV:(h + 1) * DN_DV]) for h in range(DN_HEADS)]
    mix = jnp.concatenate(mix, axis=-1)
    x_new = x_ref[0] + mod_ref[2:3, :] * _mm(mix, w_ref[...])
    xo_ref[0] = x_new
    hm = _norm_mod(x_new, g2_ref[...], mod_ref[3:4, :], mod_ref[4:5, :])
    hm_ref[0] = hm.astype(hm_ref.dtype)
    rw = rw_ref[...]
    hm_hi, rw_hi = hm.astype(MXU_DT), rw.astype(MXU_DT)
    hm_lo, rw_lo = hm - hm_hi.astype(F32), rw - rw_hi.astype(F32)
    logits = (_mm(hm_hi, rw_hi) + (_mm(hm_hi, rw_lo) + _mm(hm_lo, rw_hi))) + rb_ref[...]
    lane = lax.broadcasted_iota(jnp.int32, logits.shape, 1)
    vals = jnp.zeros_like(logits)
    idxs = jnp.zeros(logits.shape, jnp.int32)
    work = jnp.where(lane < N_EXPERTS, logits, -jnp.inf)
    for kk in range(TOP_K):
        mx = jnp.max(work, axis=-1, keepdims=True)
        am = jnp.min(jnp.where(work == mx, lane, LANES), axis=-1, keepdims=True)
        vals = jnp.where(lane == kk, mx, vals)
        idxs = jnp.where(lane == kk, am, idxs)
        work = jnp.where(lane == am, -jnp.inf, work)
    ex = jnp.where(lane < TOP_K, jnp.exp(vals - jnp.max(jnp.where(lane < TOP_K, vals, -jnp.inf), axis=-1, keepdims=True)), 0.0)
    tv_ref[0] = ex / jnp.sum(ex, axis=-1, keepdims=True)
    ti_ref[0] = idxs


def out_projection(even, mixer_in, gain, x, mod, g2, w_out, router_w, router_b):
    b, t, d = x.shape
    tok = lambda wdt, blk: pl.BlockSpec((1, TM, wdt), lambda i, j: (i, j, blk))
    if even:
        a, hf, hb, proj = mixer_in
        ins = [a, hf, hb, proj]
        specs = [tok(NA_W, 0), tok(ML_V_W, 0), tok(ML_V_W, 0), tok(ML_V_W, E_MO // ML_V_W)]
    else:
        of, ob, proj = mixer_in
        ins = [of, ob, proj]
        specs = [tok(DN_V_W, 0), tok(DN_V_W, 0), tok(DN_V_W, O_Z // DN_V_W)]
    rw = jnp.zeros((d, LANES), F32).at[:, :N_EXPERTS].set(router_w.astype(F32))
    rb = jnp.zeros((1, LANES), F32).at[0, :N_EXPERTS].set(router_b.astype(F32))
    const = lambda shape: pl.BlockSpec(shape, lambda i, j: (0,) * len(shape))
    ins += [gain.astype(F32).reshape(1, LANES), x, mod, g2.reshape(1, d), w_out.astype(MXU_DT), rw, rb]
    specs += [const((1, LANES)), tok(d, 0),
              pl.BlockSpec((None, None, 6, d), lambda i, j: (i, jnp.minimum(j, 1), 0, 0)),
              const((1, d)), const(w_out.shape), const((d, LANES)), const((1, LANES))]
    return pl.pallas_call(
        functools.partial(_outproj_kernel, even=even),
        grid=(b, t // TM),
        in_specs=specs,
        out_specs=[tok(d, 0), tok(d, 0), tok(LANES, 0), tok(LANES, 0)],
        out_shape=[jax.ShapeDtypeStruct((b, t, d), F32), jax.ShapeDtypeStruct((b, t, d), F32),
                   jax.ShapeDtypeStruct((b, t, LANES), F32), jax.ShapeDtypeStruct((b, t, LANES), jnp.int32)],
        compiler_params=_cp(("parallel", "arbitrary")),
        name="out_projection_even" if even else "out_projection_odd",
    )(*ins)


MOE_GRP = 2 * LANES


def _swiglu_perm():
    p = np.zeros((MOE_GRP, MOE_GRP), np.float32)
    j = np.arange(LANES)
    p[2 * j, j] = 1.0
    p[2 * j + 1, LANES + j] = 1.0
    return p


def _swiglu_col_order(n):
    g = np.arange(n // MOE_GRP)[:, None] * MOE_GRP
    j = np.arange(LANES)[None, :]
    return np.concatenate([g + 2 * j, g + 2 * j + 1], axis=1).reshape(-1)


def _experts_kernel(be_ref, nb_ref, x_ref, w1_ref, b1_ref, w2_ref, b2_ref, perm_ref, o_ref, w1s_ref, w2s_ref):
    i = pl.program_id(0)
    n_grp = w1s_ref.shape[1] // MOE_GRP

    @pl.when((i == 0) | (be_ref[i] != be_ref[jnp.maximum(i - 1, 0)]))
    def _():
        for g in range(n_grp):
            cols = slice(g * MOE_GRP, (g + 1) * MOE_GRP)
            w1s_ref[:, cols] = jnp.dot(w1_ref[0, :, cols].astype(MXU_DT), perm_ref[...],
                                       preferred_element_type=F32).astype(MXU_DT)
        w2s_ref[...] = w2_ref[0].astype(MXU_DT)

    @pl.when(i < nb_ref[0])
    def _():
        x = x_ref[...].astype(MXU_DT)
        acts = []
        for g in range(n_grp):
            cols = slice(g * MOE_GRP, (g + 1) * MOE_GRP)
            hid = jnp.dot(x, w1s_ref[:, cols], preferred_element_type=F32) + b1_ref[0, :, cols]
            glu = jnp.minimum(hid[:, :LANES], SWIGLU_LIMIT)
            lin = jnp.clip(hid[:, LANES:], -SWIGLU_LIMIT, SWIGLU_LIMIT)
            acts.append((glu * jax.nn.sigmoid(SWIGLU_ALPHA * glu) * (lin + 1.0)).astype(MXU_DT))
        act = jnp.concatenate(acts, axis=-1)
        o_ref[...] = (jnp.dot(act, w2s_ref[...], preferred_element_type=F32) + b2_ref[0]).astype(o_ref.dtype)

    @pl.when(i >= nb_ref[0])
    def _():
        o_ref[...] = jnp.zeros_like(o_ref)


def experts_ffn(xs, block_exp, n_used, layer, w1, b1p, w2, b2):
    n_slots, d = xs.shape
    n_blocks = n_slots // MOE_BM
    dh2 = w1.shape[-1]
    wspec = lambda shape: pl.BlockSpec((None, 1) + shape, lambda i, be, nb: (layer, be[i], 0, 0))
    bspec = lambda shape: pl.BlockSpec((1,) + shape, lambda i, be, nb: (be[i], 0, 0))
    return pl.pallas_call(
        _experts_kernel,
        grid_spec=pltpu.PrefetchScalarGridSpec(
            num_scalar_prefetch=2,
            grid=(n_blocks,),
            in_specs=[pl.BlockSpec((MOE_BM, d), lambda i, be, nb: (jnp.minimum(i, nb[0] - 1), 0)),
                      wspec((d, dh2)), bspec((1, dh2)), wspec((dh2 // 2, d)), bspec((1, d)),
                      pl.BlockSpec((MOE_GRP, MOE_GRP), lambda i, be, nb: (0, 0))],
            out_specs=pl.BlockSpec((MOE_BM, d), lambda i, be, nb: (i, 0)),
            scratch_shapes=[pltpu.VMEM((d, dh2), MXU_DT), pltpu.VMEM((dh2 // 2, d), MXU_DT)],
        ),
        out_shape=jax.ShapeDtypeStruct((n_slots, d), MXU_DT),
        compiler_params=_cp(("arbitrary",)),
        name="experts_ffn",
    )(block_exp, n_used, xs, w1, b1p, w2, b2, jnp.asarray(_swiglu_perm(), MXU_DT))


def _combine_kernel(x_ref, y_ref, gate_ref, mod_ref, o_ref):
    gate = gate_ref[0]
    f = y_ref[0, 0].astype(F32) * gate[:, 0:1]
    for kk in range(1, TOP_K):
        f = f + y_ref[0, kk].astype(F32) * gate[:, kk:kk + 1]
    o_ref[0] = x_ref[0] + mod_ref[5:6, :] * f


def moe_combine(x, yg, gate, mod):
    b, t, d = x.shape
    return pl.pallas_call(
        _combine_kernel,
        grid=(b, t // TM),
        in_specs=[pl.BlockSpec((1, TM, d), lambda i, j: (i, j, 0)),
                  pl.BlockSpec((1, TOP_K, TM, d), lambda i, j: (i, 0, j, 0)),
                  pl.BlockSpec((1, TM, LANES), lambda i, j: (i, j, 0)),
                  pl.BlockSpec((None, None, 6, d), lambda i, j: (i, jnp.minimum(j, 1), 0, 0))],
        out_specs=pl.BlockSpec((1, TM, d), lambda i, j: (i, j, 0)),
        out_shape=jax.ShapeDtypeStruct((b, t, d), F32),
        compiler_params=_cp(("parallel", "arbitrary")),
        name="moe_combine",
    )(x, yg, gate, mod)


def moe_layer(x_new, hm, gate, top_idx, mod, layer, w1, b1, w2, b2):
    b, t, d = x_new.shape
    n_tok = b * t
    n_asg = n_tok * TOP_K
    flat_exp = top_idx[:, :, :TOP_K].transpose(0, 2, 1).reshape(n_asg)
    tok_of_asg = (jnp.arange(b, dtype=jnp.int32)[:, None, None] * t + jnp.zeros((1, TOP_K, 1), jnp.int32)
                  + jnp.arange(t, dtype=jnp.int32)[None, None, :]).reshape(n_asg)
    cb = 256
    onehot = (flat_exp[:, None] == jnp.arange(N_EXPERTS, dtype=jnp.int32)[None, :]).astype(F32)
    onehot = onehot.reshape(n_asg // cb, cb, N_EXPERTS)
    within = jnp.einsum('ij,bjk->bik', jnp.tril(jnp.ones((cb, cb), F32)), onehot)
    blk_tot = within[:, -1, :]
    blk_off = jnp.cumsum(blk_tot, axis=0) - blk_tot
    counts = (blk_off[-1] + blk_tot[-1]).astype(jnp.int32)
    rank = jnp.sum((within - onehot + blk_off[:, None, :]) * onehot, axis=-1).reshape(n_asg).astype(jnp.int32)
    padded = (counts + MOE_BM - 1) // MOE_BM * MOE_BM
    pad_end = jnp.cumsum(padded)
    dest = (pad_end - padded)[flat_exp] + rank
    n_blocks = -(-n_asg // MOE_BM) + N_EXPERTS
    n_slots = n_blocks * MOE_BM
    slot_tok = (jnp.arange(n_slots, dtype=jnp.int32) % n_tok).at[dest].set(tok_of_asg, unique_indices=True,
                                                                           mode='promise_in_bounds')
    block_start = jnp.arange(n_blocks, dtype=jnp.int32) * MOE_BM
    block_exp = jnp.minimum(jnp.sum((pad_end[None, :] <= block_start[:, None]).astype(jnp.int32), axis=1),
                            N_EXPERTS - 1)
    n_used = (pad_end[-1] // MOE_BM).astype(jnp.int32).reshape(1)
    xs = hm.reshape(n_tok, d)[slot_tok]
    b1p = b1.astype(F32)[:, _swiglu_col_order(b1.shape[-1])][:, None, :]
    y = experts_ffn(xs, block_exp, n_used, layer, w1, b1p, w2, b2[:, None, :].astype(F32))
    yg = y[dest].reshape(b, TOP_K, t, d)
    return moe_combine(x_new, yg, gate, mod)


def _pad_cols(w, n):
    return jnp.pad(w, ((0, 0), (0, n - w.shape[1]))).astype(MXU_DT)


N_STREAMS = 2


def kernel(x, c, ctx, c_ctx, ada_w, ada_b, norm1_g, norm2_g, ev_w_in, ev_w_out, na_q_gain, na_k_gain, na_rpb,
           ml_gate_b, ml_out_gain, od_w_in, od_w_out, dn_conv_w, dn_a_log, dn_dt_bias, dn_out_gain, router_w,
           router_b, exp_w1, exp_b1, exp_w2, exp_b2):
    b = x.shape[0]
    assert b % N_STREAMS == 0
    g = b // N_STREAMS
    outs = [_trunk(x[i:i + g], c[i:i + g], ctx[i:i + g], c_ctx, ada_w, ada_b, norm1_g, norm2_g, ev_w_in, ev_w_out,
                   na_q_gain, na_k_gain, na_rpb, ml_gate_b, ml_out_gain, od_w_in, od_w_out, dn_conv_w, dn_a_log,
                   dn_dt_bias, dn_out_gain, router_w, router_b, exp_w1, exp_b1, exp_w2, exp_b2)
            for i in range(0, b, g)]
    return jnp.concatenate(outs, axis=0)


def _trunk(x, c, ctx, c_ctx, ada_w, ada_b, norm1_g, norm2_g, ev_w_in, ev_w_out, na_q_gain, na_k_gain, na_rpb,
           ml_gate_b, ml_out_gain, od_w_in, od_w_out, dn_conv_w, dn_a_log, dn_dt_bias, dn_out_gain, router_w,
           router_b, exp_w1, exp_b1, exp_w2, exp_b2):
    b, s, d = x.shape
    depth = ada_w.shape[0]
    t = CTX_LEN + s
    xs = jnp.concatenate([ctx, x], axis=1)
    cond = jnp.zeros((16, d), F32).at[:b].set(c).at[b].set(c_ctx)
    mods = adaln_all(cond, ada_w, ada_b)
    mod_lat = mods[:, :b].reshape(depth, b, 1, 6, d)
    mod_ctx = jnp.broadcast_to(mods[:, b].reshape(depth, 1, 1, 6, d), (depth, b, 1, 6, d))
    mod_all = jnp.concatenate([mod_ctx, mod_lat], axis=2)
    cos, sin = rope_tables(t)
    for layer in range(depth):
        mod = mod_all[layer]
        if layer % 2 == 0:
            e = layer // 2
            proj = in_projection(xs, mod, norm1_g[layer], _pad_cols(ev_w_in[e], EVEN_PAD), 640)
            a = neighbourhood_attention(proj, na_q_gain[e], na_k_gain[e], na_bias_tables(na_rpb[e]))
            gates_t = proj[:, :, E_MG:E_MG + 16].transpose(0, 2, 1)
            hf = mlstm_direction(proj, gates_t, ml_gate_b[e], cos, sin, False)
            hb = mlstm_direction(proj, gates_t, ml_gate_b[e], cos, sin, True)
            mixer_in, gain, w_out = (a, hf, hb, proj), ml_out_gain[e], ev_w_out[e]
        else:
            o = layer // 2
            proj = in_projection(xs, mod, norm1_g[layer], _pad_cols(od_w_in[o], ODD_PAD), 1056)
            qkv = dn_conv(proj, dn_conv_w[o])
            gates_t = proj[:, :, O_A:O_A + 16].transpose(0, 2, 1)
            of = gdn_direction(qkv, proj, gates_t, dn_a_log[o], dn_dt_bias[o], False)
            ob = gdn_direction(qkv, proj, gates_t, dn_a_log[o], dn_dt_bias[o], True)
            mixer_in, gain, w_out = (of, ob, proj), dn_out_gain[o], od_w_out[o]
        x_new, hm, gate, top_idx = out_projection(layer % 2 == 0, mixer_in, gain, xs, mod, norm2_g[layer], w_out,
                                                  router_w[layer], router_b[layer])
        xs = moe_layer(x_new, hm, gate, top_idx, mod, layer, exp_w1, exp_b1[layer], exp_w2, exp_b2[layer])
    return xs[:, CTX_LEN:]
```

```python
import functools

import numpy as np
import jax
import jax.numpy as jnp
from jax import lax
from jax.experimental import pallas as pl
from jax.experimental.pallas import tpu as pltpu

F32 = jnp.float32
MXU_DT = jnp.bfloat16
EPS = 1e-6
NEG = -1e30

D_MODEL = 1024
GRID_W = 64
CTX_LEN = 256
TM = 256
LANES = 128

NA_HEADS, NA_DH = 8, 64
NA_W = NA_HEADS * NA_DH
NA_WIN_R, NA_WIN_C = 8, 16
NA_RB = 4
NA_QB = NA_RB * GRID_W
NA_KROWS = NA_RB + NA_WIN_R - 1
NA_KU = NA_KROWS * GRID_W

ML_HEADS, ML_DQK, ML_DV, ML_CHUNK = 4, 64, 128, 64
ML_QK_W, ML_V_W = ML_HEADS * ML_DQK, ML_HEADS * ML_DV
ROPE_THETA = 10000.0
EVEN_IN = 3 * NA_W + 2 * ML_QK_W + 2 * ML_V_W + 4 * ML_HEADS
EVEN_PAD = 3200
E_Q, E_K, E_V = 0, NA_W, 2 * NA_W
E_MQ = 3 * NA_W
E_MK = E_MQ + ML_QK_W
E_MV = E_MK + ML_QK_W
E_MO = E_MV + ML_V_W
E_MG = E_MO + ML_V_W

DN_HEADS, DN_DK, DN_DV, DN_CHUNK, DN_CONV = 8, 128, 128, 64, 5
DN_QK_W, DN_V_W = DN_HEADS * DN_DK, DN_HEADS * DN_DV
DN_CONV_CH = 2 * DN_QK_W + DN_V_W
ODD_IN = DN_CONV_CH + DN_V_W + 4 * DN_HEADS
ODD_PAD = 4224
O_Z = DN_CONV_CH
O_A = O_Z + DN_V_W

N_EXPERTS, TOP_K, D_EXPERT = 32, 4, 1024
SWIGLU_LIMIT, SWIGLU_ALPHA = 7.0, 1.702
MOE_BM = 512

VMEM_LIMIT = 56 * 1024 * 1024


def _cp(sem):
    return pltpu.CompilerParams(dimension_semantics=sem, vmem_limit_bytes=VMEM_LIMIT)


def _mm(a, b):
    return jnp.dot(a.astype(MXU_DT), b.astype(MXU_DT), preferred_element_type=F32)


def _mm_nt(a, b):
    return lax.dot_general(a.astype(MXU_DT), b.astype(MXU_DT), (((1,), (1,)), ((), ())),
                           preferred_element_type=F32)


def _mm_tn(a, b):
    return lax.dot_general(a.astype(MXU_DT), b.astype(MXU_DT), (((0,), (0,)), ((), ())),
                           preferred_element_type=F32)


def _bmm(a, b):
    return jnp.einsum('bij,bjk->bik', a.astype(MXU_DT), b.astype(MXU_DT), preferred_element_type=F32)


def _bmm_nt(a, b):
    return jnp.einsum('bik,bjk->bij', a.astype(MXU_DT), b.astype(MXU_DT), preferred_element_type=F32)


def _bmm_tn(a, b):
    return jnp.einsum('bki,bkj->bij', a.astype(MXU_DT), b.astype(MXU_DT), preferred_element_type=F32)


def _silu(x):
    return x * jax.nn.sigmoid(x)


def _adaln_kernel(c_ref, w_ref, b_ref, o_ref):
    o_ref[0] = _mm(_silu(c_ref[...]), w_ref[0]) + b_ref[0]


def adaln_all(cond, ada_w, ada_b):
    depth, d, n = ada_w.shape
    tn = 1536
    return pl.pallas_call(
        _adaln_kernel,
        grid=(depth, n // tn),
        in_specs=[pl.BlockSpec(cond.shape, lambda l, j: (0, 0)),
                  pl.BlockSpec((1, d, tn), lambda l, j: (l, 0, j)),
                  pl.BlockSpec((1, 1, tn), lambda l, j: (l, 0, j))],
        out_specs=pl.BlockSpec((1, cond.shape[0], tn), lambda l, j: (l, 0, j)),
        out_shape=jax.ShapeDtypeStruct((depth, cond.shape[0], n), F32),
        compiler_params=_cp(("arbitrary", "arbitrary")),
        name="adaln",
    )(cond, ada_w, ada_b.reshape(depth, 1, n))


def _norm_mod(x, gain, shift, scale):
    y = x * lax.rsqrt(jnp.mean(x * x, axis=-1, keepdims=True) + EPS)
    return (y * gain) * (1.0 + scale) + shift


def _inproj_kernel(x_ref, mod_ref, g_ref, w_ref, o_ref, *, n_chunk):
    h = _norm_mod(x_ref[0], g_ref[...], mod_ref[0:1, :], mod_ref[1:2, :]).astype(MXU_DT)
    for n0 in range(0, o_ref.shape[-1], n_chunk):
        o_ref[0, :, n0:n0 + n_chunk] = jnp.dot(h, w_ref[:, n0:n0 + n_chunk], preferred_element_type=F32)


def in_projection(x, mod, gain, w, n_chunk):
    b, t, d = x.shape
    n = w.shape[1]
    return pl.pallas_call(
        functools.partial(_inproj_kernel, n_chunk=n_chunk),
        grid=(b, t // TM),
        in_specs=[pl.BlockSpec((1, TM, d), lambda i, j: (i, j, 0)),
                  pl.BlockSpec((None, None, 6, d), lambda i, j: (i, jnp.minimum(j, 1), 0, 0)),
                  pl.BlockSpec((1, d), lambda i, j: (0, 0)),
                  pl.BlockSpec((d, n), lambda i, j: (0, 0))],
        out_specs=pl.BlockSpec((1, TM, n), lambda i, j: (i, j, 0)),
        out_shape=jax.ShapeDtypeStruct((b, t, n), F32),
        compiler_params=_cp(("parallel", "arbitrary")),
        name="in_projection",
    )(x, mod, gain.reshape(1, d), w)


def _na_block_start(j):
    return jnp.clip(NA_RB * (j - 1) - NA_WIN_R // 2, 0, GRID_W - NA_KROWS)


def na_bias_tables(rpb):
    rows = GRID_W
    n_dr = 2 * NA_WIN_R - 1
    cq = np.arange(GRID_W)[:, None]
    ck = np.arange(GRID_W)[None, :]
    c0 = np.clip(cq - NA_WIN_C // 2, 0, GRID_W - NA_WIN_C)
    ok_c = (ck >= c0) & (ck < c0 + NA_WIN_C)
    ic = np.clip(ck - cq + NA_WIN_C - 1, 0, 2 * NA_WIN_C - 2)
    tiles = jnp.where(ok_c[None, None], rpb.astype(F32)[:, :, ic], NEG)
    tiles = jnp.concatenate([tiles, jnp.full_like(tiles[:, :1], NEG)], axis=1)
    idx = np.full((4, NA_RB, NA_KROWS), n_dr, np.int32)
    for var, rb in enumerate((0, 1, rows // NA_RB - 1)):
        start = int(np.clip(NA_RB * rb - NA_WIN_R // 2, 0, rows - NA_KROWS))
        for qi in range(NA_RB):
            rq = NA_RB * rb + qi
            r0 = int(np.clip(rq - NA_WIN_R // 2, 0, rows - NA_WIN_R))
            for kj in range(NA_KROWS):
                rk = start + kj
                if r0 <= rk < r0 + NA_WIN_R:
                    idx[var, qi, kj] = rk - rq + NA_WIN_R - 1
    full = tiles[:, idx]
    return full.transpose(0, 1, 2, 4, 3, 5).reshape(rpb.shape[0], 4, NA_QB, NA_KU)


def _na_kernel(q_ref, k_ref, v_ref, qg_ref, kg_ref, bias_ref, o_ref, kn_ref):
    j = pl.program_id(2)
    is_h0 = lax.broadcasted_iota(jnp.int32, (1, LANES), 1) < NA_DH

    def headnorm(x, gain):
        x2 = x * x
        s0 = jnp.sum(jnp.where(is_h0, x2, 0.0), axis=-1, keepdims=True)
        s1 = jnp.sum(jnp.where(is_h0, 0.0, x2), axis=-1, keepdims=True)
        ms = jnp.where(is_h0, s0, s1) * (1.0 / NA_DH)
        return (x * lax.rsqrt(ms + EPS)) * gain

    @pl.when(j == 0)
    def _():
        kn_ref[...] = headnorm(k_ref[0], kg_ref[...]).astype(MXU_DT)

    qn = headnorm(q_ref[0], qg_ref[...]) * (NA_DH ** -0.5)
    off = pl.multiple_of(CTX_LEN + _na_block_start(j) * GRID_W, GRID_W)
    k_loc = kn_ref[pl.ds(off, NA_KU), :]
    k_ctx = kn_ref[0:CTX_LEN, :]
    v_loc = v_ref[0, pl.ds(off, NA_KU), :].astype(MXU_DT)
    v_ctx = v_ref[0, 0:CTX_LEN, :].astype(MXU_DT)
    outs = []
    for h in range(2):
        qm = jnp.where(is_h0 if h == 0 else jnp.logical_not(is_h0), qn, 0.0)
        s_loc = _mm_nt(qm, k_loc) + bias_ref[h, 0]
        s_ctx = _mm_nt(qm, k_ctx)
        m = jnp.maximum(jnp.max(s_loc, axis=-1, keepdims=True), jnp.max(s_ctx, axis=-1, keepdims=True))
        p_loc = jnp.exp(s_loc - m)
        p_ctx = jnp.exp(s_ctx - m)
        den = jnp.sum(p_loc, axis=-1, keepdims=True) + jnp.sum(p_ctx, axis=-1, keepdims=True)
        outs.append((_mm(p_loc, v_loc) + _mm(p_ctx, v_ctx)) / den)
    o_ref[0] = jnp.where(is_h0, outs[0], outs[1])


def neighbourhood_attention(proj, q_gain, k_gain, bias):
    b, t, _ = proj.shape
    nq = t // NA_QB
    n_rb = GRID_W // NA_RB

    def bias_idx(i, hp, j):
        var = jnp.where(j == 0, 3, jnp.where(j == 1, 0, jnp.where(j == n_rb, 2, 1)))
        return (hp, var, 0, 0)

    tile2 = lambda g: jnp.tile(g.astype(F32), 2).reshape(1, LANES)
    return pl.pallas_call(
        _na_kernel,
        grid=(b, NA_HEADS // 2, nq),
        in_specs=[pl.BlockSpec((1, NA_QB, LANES), lambda i, hp, j: (i, j, E_Q // LANES + hp)),
                  pl.BlockSpec((1, t, LANES), lambda i, hp, j: (i, 0, E_K // LANES + hp)),
                  pl.BlockSpec((1, t, LANES), lambda i, hp, j: (i, 0, E_V // LANES + hp)),
                  pl.BlockSpec((1, LANES), lambda i, hp, j: (0, 0)),
                  pl.BlockSpec((1, LANES), lambda i, hp, j: (0, 0)),
                  pl.BlockSpec((2, 1, NA_QB, NA_KU), bias_idx)],
        out_specs=pl.BlockSpec((1, NA_QB, LANES), lambda i, hp, j: (i, j, hp)),
        out_shape=jax.ShapeDtypeStruct((b, t, NA_W), F32),
        scratch_shapes=[pltpu.VMEM((t, LANES), MXU_DT)],
        compiler_params=_cp(("parallel", "arbitrary", "arbitrary")),
        name="neighbourhood_attention",
    )(proj, proj, proj, tile2(q_gain), tile2(k_gain), bias)


def _scan_tile(i, n_tiles, reverse):
    if not reverse:
        return i
    return jnp.where(i == 0, 0, n_tiles - i)


def _tri_masks(n, reverse):
    r = lax.broadcasted_iota(jnp.int32, (n, n), 0)
    c = lax.broadcasted_iota(jnp.int32, (n, n), 1)
    if reverse:
        return c >= r, c > r, r >= c
    return c <= r, c < r, r <= c


def _cumsum_both(x_col, x_row, incl, incl_t):
    cum_col = jnp.sum(jnp.where(incl, x_row, 0.0), axis=1, keepdims=True)
    cum_row = jnp.sum(jnp.where(incl_t, x_col, 0.0), axis=0, keepdims=True)
    return cum_col, cum_row


def _rope(x, cos, sin):
    w = x.shape[-1]
    lane = lax.broadcasted_iota(jnp.int32, (1, w), 1)
    first = (lane % ML_DQK) < ML_DQK // 2
    swapped = jnp.where(first, pltpu.roll(x, w - ML_DQK // 2, 1), pltpu.roll(x, ML_DQK // 2, 1))
    return x * cos + swapped * sin


def _mlstm_kernel(q_ref, k_ref, v_ref, g_ref, gt_ref, gbr_ref, gbc_ref, cos_ref, sin_ref, o_ref,
                  c_ref, n_ref, m_ref, *, reverse):
    i = pl.program_id(1)
    d = 1 if reverse else 0

    @pl.when(i == 0)
    def _():
        c_ref[...] = jnp.zeros_like(c_ref)
        n_ref[...] = jnp.zeros_like(n_ref)
        m_ref[...] = jnp.zeros_like(m_ref)

    q = _rope(q_ref[0], cos_ref[...], sin_ref[...])
    k = _rope(k_ref[0] * (ML_DQK ** -0.5), cos_ref[...], sin_ref[...])
    g_col = g_ref[0] + gbr_ref[...]
    g_row = gt_ref[0] + gbc_ref[...]
    incl, _, incl_t = _tri_masks(ML_CHUNK, reverse)
    is_h0 = lax.broadcasted_iota(jnp.int32, (1, LANES), 1) < ML_DQK
    nc = TM // ML_CHUNK
    nh = ML_HEADS
    order = range(nc - 1, -1, -1) if reverse else range(nc)
    tokc = lambda ci: slice(ci * ML_CHUNK, (ci + 1) * ML_CHUNK)
    pair = lambda h: slice((h // 2) * LANES, (h // 2 + 1) * LANES)
    hmask = lambda h: is_h0 if h % 2 == 0 else jnp.logical_not(is_h0)
    stack = lambda fn: jnp.stack([fn(h, ci) for h in range(nh) for ci in range(nc)], axis=0)
    by_head = lambda a: a.reshape((nh, nc) + a.shape[1:])
    flat = lambda parts: jnp.stack(parts, axis=1).reshape((nh * nc,) + parts[0].shape[1:])
    qm = stack(lambda h, ci: jnp.where(hmask(h), q[tokc(ci), pair(h)], 0.0))
    km = stack(lambda h, ci: jnp.where(hmask(h), k[tokc(ci), pair(h)], 0.0))
    v = stack(lambda h, ci: v_ref[0, tokc(ci), h * ML_DV:(h + 1) * ML_DV])
    li_c = stack(lambda h, ci: g_col[tokc(ci), 8 * d + h:8 * d + h + 1])
    lf_c = jax.nn.log_sigmoid(stack(lambda h, ci: g_col[tokc(ci), 8 * d + 4 + h:8 * d + 5 + h]))
    li_r = stack(lambda h, ci: g_row[8 * d + h:8 * d + h + 1, tokc(ci)])
    lf_r = jax.nn.log_sigmoid(stack(lambda h, ci: g_row[8 * d + 4 + h:8 * d + 5 + h, tokc(ci)]))
    cum_c = jnp.sum(jnp.where(incl, lf_r, 0.0), axis=-1, keepdims=True)
    cum_r = jnp.sum(jnp.where(incl_t, lf_c, 0.0), axis=-2, keepdims=True)
    log_d = jnp.where(incl, cum_c - cum_r + li_r, NEG)
    m_loc = jnp.max(log_d, axis=-1, keepdims=True)
    cum_last = jnp.sum(lf_r, axis=-1, keepdims=True)
    log_w = cum_last - cum_c + li_c
    m_w = jnp.max(log_w, axis=-2, keepdims=True)
    cl4, mw4 = by_head(cum_last), by_head(m_w)
    m = m_ref[...][:, :, 0:1]
    m_prev, m_new = [None] * nc, [None] * nc
    for ci in order:
        m_prev[ci] = m
        m = jnp.maximum(cl4[:, ci] + m, mw4[:, ci])
        m_new[ci] = m
    m_ref[...] = jnp.broadcast_to(m, m_ref.shape)
    m_prev, m_new = flat(m_prev), flat(m_new)
    m_inter = cum_c + m_prev
    m_t = jnp.maximum(m_loc, m_inter)
    sg = _bmm_nt(qm, km) * jnp.exp(log_d - m_t)
    w_inter = jnp.exp(m_inter - m_t)
    kw = km * jnp.exp(log_w - m_new)
    upd_c, upd_n = by_head(_bmm_tn(kw, v)), by_head(jnp.sum(kw, axis=-2, keepdims=True))
    decay = by_head(jnp.exp(cum_last + m_prev - m_new))
    c_mat, n_vec = c_ref[...], n_ref[...]
    c_prev, n_prev = [None] * nc, [None] * nc
    for ci in order:
        c_prev[ci], n_prev[ci] = c_mat, n_vec
        c_mat = decay[:, ci] * c_mat + upd_c[:, ci]
        n_vec = decay[:, ci] * n_vec + upd_n[:, ci]
    c_ref[...] = c_mat
    n_ref[...] = n_vec
    num = _bmm(sg, v) + w_inter * _bmm(qm, flat(c_prev))
    den = jnp.sum(sg, axis=-1, keepdims=True) + w_inter * jnp.sum(qm * flat(n_prev), axis=-1, keepdims=True)
    h_out = by_head(num / jnp.maximum(jnp.abs(den), jnp.exp(-m_t)))
    for h in range(nh):
        for ci in range(nc):
            o_ref[0, 0, tokc(ci), h * ML_DV:(h + 1) * ML_DV] = h_out[h, ci]


def mlstm_direction(proj, gates_t, gate_b, cos, sin, reverse):
    b, t, _ = proj.shape
    n_tiles = t // TM
    tile = functools.partial(_scan_tile, n_tiles=n_tiles, reverse=reverse)
    gb = gate_b.astype(F32).reshape(16)
    gb_row = jnp.zeros((1, LANES), F32).at[0, :16].set(gb)
    gb_col = gb.reshape(16, 1)
    return pl.pallas_call(
        functools.partial(_mlstm_kernel, reverse=reverse),
        grid=(b, n_tiles),
        in_specs=[pl.BlockSpec((1, TM, ML_QK_W), lambda bi, i: (bi, tile(i), E_MQ // ML_QK_W)),
                  pl.BlockSpec((1, TM, ML_QK_W), lambda bi, i: (bi, tile(i), E_MK // ML_QK_W)),
                  pl.BlockSpec((1, TM, ML_V_W), lambda bi, i: (bi, tile(i), E_MV // ML_V_W)),
                  pl.BlockSpec((1, TM, LANES), lambda bi, i: (bi, tile(i), E_MG // LANES)),
                  pl.BlockSpec((1, 16, TM), lambda bi, i: (bi, 0, tile(i))),
                  pl.BlockSpec((1, LANES), lambda bi, i: (0, 0)),
                  pl.BlockSpec((16, 1), lambda bi, i: (0, 0)),
                  pl.BlockSpec((TM, ML_QK_W), lambda bi, i: (tile(i), 0)),
                  pl.BlockSpec((TM, ML_QK_W), lambda bi, i: (tile(i), 0))],
        out_specs=pl.BlockSpec((1, 1, TM, ML_V_W), lambda bi, i: (0, bi, tile(i), 0)),
        out_shape=jax.ShapeDtypeStruct((1, b, t, ML_V_W), F32),
        scratch_shapes=[pltpu.VMEM((ML_HEADS, LANES, ML_DV), F32),
                        pltpu.VMEM((ML_HEADS, 1, LANES), F32),
                        pltpu.VMEM((ML_HEADS, 1, LANES), F32)],
        compiler_params=_cp(("parallel", "arbitrary")),
        name="mlstm_bwd" if reverse else "mlstm_fwd",
    )(proj, proj, proj, proj, gates_t, gb_row, gb_col, cos, sin)[0]


def rope_tables(t):
    s = t - CTX_LEN
    tok = jnp.arange(s)
    n_freq = ML_DQK // 4
    inv = ROPE_THETA ** (-jnp.arange(n_freq, dtype=F32) / n_freq)
    ang = jnp.concatenate([(tok // GRID_W).astype(F32)[:, None] * inv,
                           (tok % GRID_W).astype(F32)[:, None] * inv], axis=-1)
    cos, sin = jnp.cos(ang), jnp.sin(ang)
    cos_h = jnp.concatenate([cos, cos], axis=-1)
    sin_h = jnp.concatenate([-sin, sin], axis=-1)
    cos_f = jnp.concatenate([jnp.ones((CTX_LEN, ML_DQK), F32), cos_h], axis=0)
    sin_f = jnp.concatenate([jnp.zeros((CTX_LEN, ML_DQK), F32), sin_h], axis=0)
    return jnp.tile(cos_f, (1, ML_HEADS)), jnp.tile(sin_f, (1, ML_HEADS))


def _dnconv_kernel(x_ref, w_ref, o_ref):
    cb = pl.program_id(1)
    x = x_ref[0]
    t = x.shape[0]
    tok = lax.broadcasted_iota(jnp.int32, (t, 1), 0)
    seg_lo = jnp.where(tok < CTX_LEN, 0, CTX_LEN)
    seg_hi = jnp.where(tok < CTX_LEN, CTX_LEN, t)
    acc = jnp.zeros_like(x)
    for j in range(DN_CONV):
        o = j - DN_CONV // 2
        xs = x if o == 0 else pltpu.roll(x, (-o) % t, 0)
        ok = (tok + o >= seg_lo) & (tok + o < seg_hi)
        acc = acc + jnp.where(ok, xs, 0.0) * w_ref[j:j + 1, :]
    y = _silu(acc)
    nrm = lax.rsqrt(jnp.sum(y * y, axis=-1, keepdims=True) + EPS)
    q_heads = DN_QK_W // LANES
    o_ref[0] = jnp.where(cb < q_heads, (y * nrm) * (DN_DK ** -0.5), jnp.where(cb < 2 * q_heads, y * nrm, y))


def dn_conv(proj, conv_w):
    b, t, _ = proj.shape
    return pl.pallas_call(
        _dnconv_kernel,
        grid=(b, DN_CONV_CH // LANES),
        in_specs=[pl.BlockSpec((1, t, LANES), lambda i, c: (i, 0, c)),
                  pl.BlockSpec((DN_CONV, LANES), lambda i, c: (0, c))],
        out_specs=pl.BlockSpec((1, t, LANES), lambda i, c: (i, 0, c)),
        out_shape=jax.ShapeDtypeStruct((b, t, DN_CONV_CH), F32),
        compiler_params=_cp(("parallel", "arbitrary")),
        name="dn_conv",
    )(proj, conv_w.astype(F32).T)


def _gdn_kernel(q_ref, k_ref, v_ref, ab_ref, at_ref, par_ref, parc_ref, o_ref, s_ref, *, reverse):
    i = pl.program_id(1)
    d = 1 if reverse else 0

    @pl.when(i == 0)
    def _():
        s_ref[...] = jnp.zeros_like(s_ref)

    incl, strict, incl_t = _tri_masks(DN_CHUNK, reverse)
    r = lax.broadcasted_iota(jnp.int32, (DN_CHUNK, DN_CHUNK), 0)
    c = lax.broadcasted_iota(jnp.int32, (DN_CHUNK, DN_CHUNK), 1)
    eye = (r == c).astype(F32)
    ab = ab_ref[0]
    g_all = -jnp.exp(par_ref[0:1, :]) * jax.nn.softplus(ab + par_ref[1:2, :])
    beta_full = jax.nn.sigmoid(ab)
    gt_all = -jnp.exp(parc_ref[:, 0:1]) * jax.nn.softplus(at_ref[0] + parc_ref[:, 1:2])

    nc = TM // DN_CHUNK
    nh = DN_HEADS
    tokc = lambda ci: slice(ci * DN_CHUNK, (ci + 1) * DN_CHUNK)
    hl = lambda h: slice(h * LANES, (h + 1) * LANES)
    stack = lambda fn: jnp.stack([fn(h, ci) for h in range(nh) for ci in range(nc)], axis=0)
    q = stack(lambda h, ci: q_ref[0, tokc(ci), hl(h)])
    k = stack(lambda h, ci: k_ref[0, tokc(ci), hl(h)])
    v = stack(lambda h, ci: v_ref[0, tokc(ci), hl(h)])
    g_c = stack(lambda h, ci: g_all[tokc(ci), 8 * d + h:8 * d + h + 1])
    beta = stack(lambda h, ci: beta_full[tokc(ci), 16 + 8 * d + h:17 + 8 * d + h])
    g_r = stack(lambda h, ci: gt_all[8 * d + h:8 * d + h + 1, tokc(ci)])
    gc_c = jnp.sum(jnp.where(incl, g_r, 0.0), axis=-1, keepdims=True)
    gc_r = jnp.sum(jnp.where(incl_t, g_c, 0.0), axis=-2, keepdims=True)
    decay = jnp.exp(jnp.where(incl, gc_c - gc_r, NEG))
    kb = k * beta
    a_mat = jnp.where(strict, _bmm_nt(kb, k) * decay, 0.0)
    pw = -a_mat
    t_inv = eye + pw
    for _ in range(5):
        pw = _bmm(pw, pw)
        t_inv = t_inv + _bmm(t_inv, pw)
    egc = jnp.exp(gc_c)
    uw = _bmm(t_inv, jnp.concatenate([v * beta, kb * egc], axis=-1))
    attn = _bmm_nt(q, k) * decay
    g_last = jnp.sum(g_c, axis=-2, keepdims=True)
    kg = k * jnp.exp(g_last - gc_c)
    ng = _bmm_tn(kg, uw)
    au = _bmm(attn, uw)
    lhs = jnp.concatenate([ng[:, :, DN_DV:], q * egc - au[:, :, DN_DV:]], axis=1)
    by_head = lambda a: a.reshape((nh, nc) + a.shape[1:])
    lhs, n_mat, o2, dec = by_head(lhs), by_head(ng[:, :, :DN_DV]), by_head(au[:, :, :DN_DV]), by_head(jnp.exp(g_last))
    s_mat = s_ref[...]
    for ci in (range(nc - 1, -1, -1) if reverse else range(nc)):
        prod = _bmm(lhs[:, ci], s_mat)
        o = prod[:, DN_DK:] + o2[:, ci]
        for h in range(nh):
            o_ref[0, 0, tokc(ci), hl(h)] = o[h]
        s_mat = s_mat * dec[:, ci] + n_mat[:, ci] - prod[:, :DN_DK]
    s_ref[...] = s_mat


def gdn_direction(qkv, proj, gates_t, a_log, dt_bias, reverse):
    b, t, _ = qkv.shape
    n_tiles = t // TM
    tile = functools.partial(_scan_tile, n_tiles=n_tiles, reverse=reverse)
    par = jnp.zeros((2, LANES), F32)
    par = par.at[0, :16].set(a_log.astype(F32).reshape(16)).at[1, :16].set(dt_bias.astype(F32).reshape(16))
    parc = par[:, :16].T
    return pl.pallas_call(
        functools.partial(_gdn_kernel, reverse=reverse),
        grid=(b, n_tiles),
        in_specs=[pl.BlockSpec((1, TM, DN_QK_W), lambda bi, i: (bi, tile(i), 0)),
                  pl.BlockSpec((1, TM, DN_QK_W), lambda bi, i: (bi, tile(i), 1)),
                  pl.BlockSpec((1, TM, DN_V_W), lambda bi, i: (bi, tile(i), 2)),
                  pl.BlockSpec((1, TM, LANES), lambda bi, i: (bi, tile(i), O_A // LANES)),
                  pl.BlockSpec((1, 16, TM), lambda bi, i: (bi, 0, tile(i))),
                  pl.BlockSpec((2, LANES), lambda bi, i: (0, 0)),
                  pl.BlockSpec((16, 2), lambda bi, i: (0, 0))],
        out_specs=pl.BlockSpec((1, 1, TM, DN_V_W), lambda bi, i: (0, bi, tile(i), 0)),
        out_shape=jax.ShapeDtypeStruct((1, b, t, DN_V_W), F32),
        scratch_shapes=[pltpu.VMEM((DN_HEADS, DN_DK, DN_DV), F32)],
        compiler_params=_cp(("parallel", "arbitrary")),
        name="gdn_bwd" if reverse else "gdn_fwd",
    )(qkv, qkv, qkv, proj, gates_t, par, parc)[0]


def _headwise_rms(x, gain, width):
    parts = []
    for h0 in range(0, x.shape[-1], width):
        xh = x[:, h0:h0 + width]
        parts.append((xh * lax.rsqrt(jnp.mean(xh * xh, axis=-1, keepdims=True) + EPS)) * gain)
    return parts


def _outproj_kernel(*refs, even):
    if even:
        (a_ref, hf_ref, hb_ref, og_ref, gain_ref, x_ref, mod_ref, g2_ref, w_ref, rw_ref, rb_ref,
         xo_ref, hm_ref, tv_ref, ti_ref) = refs
        hs = _headwise_rms(hf_ref[0] + hb_ref[0], gain_ref[...], ML_DV)
        og = og_ref[0]
        mix = [a_ref[0]] + [hs[h] * jax.nn.sigmoid(og[:, h * ML_DV:(h + 1) * ML_DV]) for h in range(ML_HEADS)]
    else:
        (of_ref, ob_ref, z_ref, gain_ref, x_ref, mod_ref, g2_ref, w_ref, rw_ref, rb_ref,
         xo_ref, hm_ref, tv_ref, ti_ref) = refs
        hs = _headwise_rms(of_ref[0] + ob_ref[0], gain_ref[...], DN_DV)
        z = z_ref[0]
        mix = [hs[h] * _silu(z[:, h * DN_DV:(h + 1) * DN_DV]) for h in range(DN_HEADS)]
    mix = jnp.concatenate(mix, axis=-1)
    x_new = x_ref[0] + mod_ref[2:3, :] * _mm(mix, w_ref[...])
    xo_ref[0] = x_new
    hm = _norm_mod(x_new, g2_ref[...], mod_ref[3:4, :], mod_ref[4:5, :])
    hm_ref[0] = hm.astype(hm_ref.dtype)
    rw = rw_ref[...]
    hm_hi, rw_hi = hm.astype(MXU_DT), rw.astype(MXU_DT)
    hm_lo, rw_lo = hm - hm_hi.astype(F32), rw - rw_hi.astype(F32)
    logits = (_mm(hm_hi, rw_hi) + (_mm(hm_hi, rw_lo) + _mm(hm_lo, rw_hi))) + rb_ref[...]
    lane = lax.broadcasted_iota(jnp.int32, logits.shape, 1)
    vals = jnp.zeros_like(logits)
    idxs = jnp.zeros(logits.shape, jnp.int32)
    work = jnp.where(lane < N_EXPERTS, logits, -jnp.inf)
    for kk in range(TOP_K):
        mx = jnp.max(work, axis=-1, keepdims=True)
        am = jnp.min(jnp.where(work == mx, lane, LANES), axis=-1, keepdims=True)
        vals = jnp.where(lane == kk, mx, vals)
        idxs = jnp.where(lane == kk, am, idxs)
        work = jnp.where(lane == am, -jnp.inf, work)
    ex = jnp.where(lane < TOP_K, jnp.exp(vals - jnp.max(jnp.where(lane < TOP_K, vals, -jnp.inf), axis=-1, keepdims=True)), 0.0)
    tv_ref[0] = ex / jnp.sum(ex, axis=-1, keepdims=True)
    ti_ref[0] = idxs


def out_projection(even, mixer_in, gain, x, mod, g2, w_out, router_w, router_b):
    b, t, d = x.shape
    tok = lambda wdt, blk: pl.BlockSpec((1, TM, wdt), lambda i, j: (i, j, blk))
    if even:
        a, hf, hb, proj = mixer_in
        ins = [a, hf, hb, proj]
        specs = [tok(NA_W, 0), tok(ML_V_W, 0), tok(ML_V_W, 0), tok(ML_V_W, E_MO // ML_V_W)]
    else:
        of, ob, proj = mixer_in
        ins = [of, ob, proj]
        specs = [tok(DN_V_W, 0), tok(DN_V_W, 0), tok(DN_V_W, O_Z // DN_V_W)]
    rw = jnp.zeros((d, LANES), F32).at[:, :N_EXPERTS].set(router_w.astype(F32))
    rb = jnp.zeros((1, LANES), F32).at[0, :N_EXPERTS].set(router_b.astype(F32))
    const = lambda shape: pl.BlockSpec(shape, lambda i, j: (0,) * len(shape))
    ins += [gain.astype(F32).reshape(1, LANES), x, mod, g2.reshape(1, d), w_out.astype(MXU_DT), rw, rb]
    specs += [const((1, LANES)), tok(d, 0),
              pl.BlockSpec((None, None, 6, d), lambda i, j: (i, jnp.minimum(j, 1), 0, 0)),
              const((1, d)), const(w_out.shape), const((d, LANES)), const((1, LANES))]
    return pl.pallas_call(
        functools.partial(_outproj_kernel, even=even),
        grid=(b, t // TM),
        in_specs=specs,
        out_specs=[tok(d, 0), tok(d, 0), tok(LANES, 0), tok(LANES, 0)],
        out_shape=[jax.ShapeDtypeStruct((b, t, d), F32), jax.ShapeDtypeStruct((b, t, d), F32),
                   jax.ShapeDtypeStruct((b, t, LANES), F32), jax.ShapeDtypeStruct((b, t, LANES), jnp.int32)],
        compiler_params=_cp(("parallel", "arbitrary")),
        name="out_projection_even" if even else "out_projection_odd",
    )(*ins)


MOE_GRP = 2 * LANES


def _swiglu_perm():
    p = np.zeros((MOE_GRP, MOE_GRP), np.float32)
    j = np.arange(LANES)
    p[2 * j, j] = 1.0
    p[2 * j + 1, LANES + j] = 1.0
    return p


def _swiglu_col_order(n):
    g = np.arange(n // MOE_GRP)[:, None] * MOE_GRP
    j = np.arange(LANES)[None, :]
    return np.concatenate([g + 2 * j, g + 2 * j + 1], axis=1).reshape(-1)


def _experts_kernel(be_ref, nb_ref, x_ref, w1_ref, b1_ref, w2_ref, b2_ref, perm_ref, o_ref, w1s_ref, w2s_ref):
    i = pl.program_id(0)
    n_grp = w1s_ref.shape[1] // MOE_GRP

    @pl.when((i == 0) | (be_ref[i] != be_ref[jnp.maximum(i - 1, 0)]))
    def _():
        for g in range(n_grp):
            cols = slice(g * MOE_GRP, (g + 1) * MOE_GRP)
            w1s_ref[:, cols] = jnp.dot(w1_ref[0, :, cols].astype(MXU_DT), perm_ref[...],
                                       preferred_element_type=F32).astype(MXU_DT)
        w2s_ref[...] = w2_ref[0].astype(MXU_DT)

    @pl.when(i < nb_ref[0])
    def _():
        x = x_ref[...].astype(MXU_DT)
        acts = []
        for g in range(n_grp):
            cols = slice(g * MOE_GRP, (g + 1) * MOE_GRP)
            hid = jnp.dot(x, w1s_ref[:, cols], preferred_element_type=F32) + b1_ref[0, :, cols]
            glu = jnp.minimum(hid[:, :LANES], SWIGLU_LIMIT)
            lin = jnp.clip(hid[:, LANES:], -SWIGLU_LIMIT, SWIGLU_LIMIT)
            acts.append((glu * jax.nn.sigmoid(SWIGLU_ALPHA * glu) * (lin + 1.0)).astype(MXU_DT))
        act = jnp.concatenate(acts, axis=-1)
        o_ref[...] = (jnp.dot(act, w2s_ref[...], preferred_element_type=F32) + b2_ref[0]).astype(o_ref.dtype)

    @pl.when(i >= nb_ref[0])
    def _():
        o_ref[...] = jnp.zeros_like(o_ref)


def experts_ffn(xs, block_exp, n_used, layer, w1, b1p, w2, b2):
    n_slots, d = xs.shape
    n_blocks = n_slots // MOE_BM
    dh2 = w1.shape[-1]
    wspec = lambda shape: pl.BlockSpec((None, 1) + shape, lambda i, be, nb: (layer, be[i], 0, 0))
    bspec = lambda shape: pl.BlockSpec((1,) + shape, lambda i, be, nb: (be[i], 0, 0))
    return pl.pallas_call(
        _experts_kernel,
        grid_spec=pltpu.PrefetchScalarGridSpec(
            num_scalar_prefetch=2,
            grid=(n_blocks,),
            in_specs=[pl.BlockSpec((MOE_BM, d), lambda i, be, nb: (jnp.minimum(i, nb[0] - 1), 0)),
                      wspec((d, dh2)), bspec((1, dh2)), wspec((dh2 // 2, d)), bspec((1, d)),
                      pl.BlockSpec((MOE_GRP, MOE_GRP), lambda i, be, nb: (0, 0))],
            out_specs=pl.BlockSpec((MOE_BM, d), lambda i, be, nb: (i, 0)),
            scratch_shapes=[pltpu.VMEM((d, dh2), MXU_DT), pltpu.VMEM((dh2 // 2, d), MXU_DT)],
        ),
        out_shape=jax.ShapeDtypeStruct((n_slots, d), MXU_DT),
        compiler_params=_cp(("arbitrary",)),
        name="experts_ffn",
    )(block_exp, n_used, xs, w1, b1p, w2, b2, jnp.asarray(_swiglu_perm(), MXU_DT))


def _combine_kernel(x_ref, y_ref, gate_ref, mod_ref, o_ref):
    gate = gate_ref[0]
    f = y_ref[0, 0].astype(F32) * gate[:, 0:1]
    for kk in range(1, TOP_K):
        f = f + y_ref[0, kk].astype(F32) * gate[:, kk:kk + 1]
    o_ref[0] = x_ref[0] + mod_ref[5:6, :] * f


def moe_combine(x, yg, gate, mod):
    b, t, d = x.shape
    return pl.pallas_call(
        _combine_kernel,
        grid=(b, t // TM),
        in_specs=[pl.BlockSpec((1, TM, d), lambda i, j: (i, j, 0)),
                  pl.BlockSpec((1, TOP_K, TM, d), lambda i, j: (i, 0, j, 0)),
                  pl.BlockSpec((1, TM, LANES), lambda i, j: (i, j, 0)),
                  pl.BlockSpec((None, None, 6, d), lambda i, j: (i, jnp.minimum(j, 1), 0, 0))],
        out_specs=pl.BlockSpec((1, TM, d), lambda i, j: (i, j, 0)),
        out_shape=jax.ShapeDtypeStruct((b, t, d), F32),
        compiler_params=_cp(("parallel", "arbitrary")),
        name="moe_combine",
    )(x, yg, gate, mod)


def moe_layer(x_new, hm, gate, top_idx, mod, layer, w1, b1, w2, b2):
    b, t, d = x_new.shape
    n_tok = b * t
    n_asg = n_tok * TOP_K
    flat_exp = top_idx[:, :, :TOP_K].transpose(0, 2, 1).reshape(n_asg)
    cb = 256
    onehot = (flat_exp[:, None] == jnp.arange(N_EXPERTS, dtype=jnp.int32)[None, :]).astype(F32)
    onehot = onehot.reshape(n_asg // cb, cb, N_EXPERTS)
    within = jnp.einsum('ij,bjk->bik', jnp.tril(jnp.ones((cb, cb), F32)), onehot)
    blk_tot = within[:, -1, :]
    blk_off = jnp.cumsum(blk_tot, axis=0) - blk_tot
    counts = (blk_off[-1] + blk_tot[-1]).astype(jnp.int32)
    rank = jnp.sum((within - onehot + blk_off[:, None, :]) * onehot, axis=-1).reshape(n_asg).astype(jnp.int32)
    padded = (counts + MOE_BM - 1) // MOE_BM * MOE_BM
    pad_end = jnp.cumsum(padded)
    dest = (pad_end - padded)[flat_exp] + rank
    n_blocks = -(-n_asg // MOE_BM) + N_EXPERTS
    n_slots = n_blocks * MOE_BM
    block_start = jnp.arange(n_blocks, dtype=jnp.int32) * MOE_BM
    block_exp = jnp.minimum(jnp.sum((pad_end[None, :] <= block_start[:, None]).astype(jnp.int32), axis=1),
                            N_EXPERTS - 1)
    n_used = (pad_end[-1] // MOE_BM).astype(jnp.int32).reshape(1)
    asg_bits = (n_asg - 1).bit_length()
    assert N_EXPERTS << asg_bits < 2 ** 31
    sorted_asg = jnp.sort((flat_exp << asg_bits) + jnp.arange(n_asg, dtype=jnp.int32)) & ((1 << asg_bits) - 1)
    first_asg = jnp.cumsum(counts) - counts
    in_exp = (block_start - (pad_end - padded)[block_exp])[:, None] + jnp.arange(MOE_BM, dtype=jnp.int32)[None, :]
    is_row = in_exp < counts[block_exp][:, None]
    asg = sorted_asg[jnp.clip(first_asg[block_exp][:, None] + in_exp, 0, n_asg - 1).reshape(n_slots)]
    tok = (asg // (TOP_K * t)) * t + asg % t
    slot_tok = jnp.where(is_row.reshape(n_slots), tok, jnp.arange(n_slots, dtype=jnp.int32) % n_tok)
    xs = hm.reshape(n_tok, d)[slot_tok]
    b1p = b1.astype(F32)[:, _swiglu_col_order(b1.shape[-1])][:, None, :]
    y = experts_ffn(xs, block_exp, n_used, layer, w1, b1p, w2, b2[:, None, :].astype(F32))
    yg = y[dest].reshape(b, TOP_K, t, d)
    return moe_combine(x_new, yg, gate, mod)


def _pad_cols(w, n):
    return jnp.pad(w, ((0, 0), (0, n - w.shape[1]))).astype(MXU_DT)


N_STREAMS = 2


def kernel(x, c, ctx, c_ctx, ada_w, ada_b, norm1_g, norm2_g, ev_w_in, ev_w_out, na_q_gain, na_k_gain, na_rpb,
           ml_gate_b, ml_out_gain, od_w_in, od_w_out, dn_conv_w, dn_a_log, dn_dt_bias, dn_out_gain, router_w,
           router_b, exp_w1, exp_b1, exp_w2, exp_b2):
    b = x.shape[0]
    assert b % N_STREAMS == 0
    g = b // N_STREAMS
    outs = [_trunk(x[i:i + g], c[i:i + g], ctx[i:i + g], c_ctx, ada_w, ada_b, norm1_g, norm2_g, ev_w_in, ev_w_out,
                   na_q_gain, na_k_gain, na_rpb, ml_gate_b, ml_out_gain, od_w_in, od_w_out, dn_conv_w, dn_a_log,
                   dn_dt_bias, dn_out_gain, router_w, router_b, exp_w1, exp_b1, exp_w2, exp_b2)
            for i in range(0, b, g)]
    return jnp.concatenate(outs, axis=0)


def _trunk(x, c, ctx, c_ctx, ada_w, ada_b, norm1_g, norm2_g, ev_w_in, ev_w_out, na_q_gain, na_k_gain, na_rpb,
           ml_gate_b, ml_out_gain, od_w_in, od_w_out, dn_conv_w, dn_a_log, dn_dt_bias, dn_out_gain, router_w,
           router_b, exp_w1, exp_b1, exp_w2, exp_b2):
    b, s, d = x.shape
    depth = ada_w.shape[0]
    t = CTX_LEN + s
    xs = jnp.concatenate([ctx, x], axis=1)
    cond = jnp.zeros((16, d), F32).at[:b].set(c).at[b].set(c_ctx)
    mods = adaln_all(cond, ada_w, ada_b)
    mod_lat = mods[:, :b].reshape(depth, b, 1, 6, d)
    mod_ctx = jnp.broadcast_to(mods[:, b].reshape(depth, 1, 1, 6, d), (depth, b, 1, 6, d))
    mod_all = jnp.concatenate([mod_ctx, mod_lat], axis=2)
    cos, sin = rope_tables(t)
    for layer in range(depth):
        mod = mod_all[layer]
        if layer % 2 == 0:
            e = layer // 2
            proj = in_projection(xs, mod, norm1_g[layer], _pad_cols(ev_w_in[e], EVEN_PAD), 640)
            a = neighbourhood_attention(proj, na_q_gain[e], na_k_gain[e], na_bias_tables(na_rpb[e]))
            gates_t = proj[:, :, E_MG:E_MG + 16].transpose(0, 2, 1)
            hf = mlstm_direction(proj, gates_t, ml_gate_b[e], cos, sin, False)
            hb = mlstm_direction(proj, gates_t, ml_gate_b[e], cos, sin, True)
            mixer_in, gain, w_out = (a, hf, hb, proj), ml_out_gain[e], ev_w_out[e]
        else:
            o = layer // 2
            proj = in_projection(xs, mod, norm1_g[layer], _pad_cols(od_w_in[o], ODD_PAD), 1056)
            qkv = dn_conv(proj, dn_conv_w[o])
            gates_t = proj[:, :, O_A:O_A + 16].transpose(0, 2, 1)
            of = gdn_direction(qkv, proj, gates_t, dn_a_log[o], dn_dt_bias[o], False)
            ob = gdn_direction(qkv, proj, gates_t, dn_a_log[o], dn_dt_bias[o], True)
            mixer_in, gain, w_out = (of, ob, proj), dn_out_gain[o], od_w_out[o]
        x_new, hm, gate, top_idx = out_projection(layer % 2 == 0, mixer_in, gain, xs, mod, norm2_g[layer], w_out,
                                                  router_w[layer], router_b[layer])
        xs = moe_layer(x_new, hm, gate, top_idx, mod, layer, exp_w1, exp_b1[layer], exp_w2, exp_b2[layer])
    return xs[:, CTX_LEN:]
```

```python
import functools

import numpy as np
import jax
import jax.numpy as jnp
from jax import lax
from jax.experimental import pallas as pl
from jax.experimental.pallas import tpu as pltpu

F32 = jnp.float32
MXU_DT = jnp.bfloat16
EPS = 1e-6
NEG = -1e30

D_MODEL = 1024
GRID_W = 64
CTX_LEN = 256
TM = 256
LANES = 128
MXU_N = 256
INPROJ_CHUNK = 4 * MXU_N

NA_HEADS, NA_DH = 8, 64
NA_W = NA_HEADS * NA_DH
NA_WIN_R, NA_WIN_C = 8, 16
NA_RB = 4
NA_QB = NA_RB * GRID_W
NA_KROWS = NA_RB + NA_WIN_R - 1
NA_KU = NA_KROWS * GRID_W

ML_HEADS, ML_DQK, ML_DV, ML_CHUNK = 4, 64, 128, 64
ML_QK_W, ML_V_W = ML_HEADS * ML_DQK, ML_HEADS * ML_DV
ROPE_THETA = 10000.0
EVEN_IN = 3 * NA_W + 2 * ML_QK_W + 2 * ML_V_W + 4 * ML_HEADS
EVEN_PAD = 3200
E_Q, E_K, E_V = 0, NA_W, 2 * NA_W
E_MQ = 3 * NA_W
E_MK = E_MQ + ML_QK_W
E_MV = E_MK + ML_QK_W
E_MO = E_MV + ML_V_W
E_MG = E_MO + ML_V_W

DN_HEADS, DN_DK, DN_DV, DN_CHUNK, DN_CONV = 8, 128, 128, 64, 5
DN_QK_W, DN_V_W = DN_HEADS * DN_DK, DN_HEADS * DN_DV
DN_CONV_CH = 2 * DN_QK_W + DN_V_W
ODD_IN = DN_CONV_CH + DN_V_W + 4 * DN_HEADS
ODD_PAD = 4224
O_Z = DN_CONV_CH
O_A = O_Z + DN_V_W

N_EXPERTS, TOP_K, D_EXPERT = 32, 4, 1024
SWIGLU_LIMIT, SWIGLU_ALPHA = 7.0, 1.702
MOE_BM = 512

VMEM_LIMIT = 56 * 1024 * 1024


def _cp(sem):
    return pltpu.CompilerParams(dimension_semantics=sem, vmem_limit_bytes=VMEM_LIMIT)


def _mm(a, b):
    return jnp.dot(a.astype(MXU_DT), b.astype(MXU_DT), preferred_element_type=F32)


def _mm_nt(a, b):
    return lax.dot_general(a.astype(MXU_DT), b.astype(MXU_DT), (((1,), (1,)), ((), ())),
                           preferred_element_type=F32)


def _mm_tn(a, b):
    return lax.dot_general(a.astype(MXU_DT), b.astype(MXU_DT), (((0,), (0,)), ((), ())),
                           preferred_element_type=F32)


def _bmm(a, b):
    return jnp.einsum('bij,bjk->bik', a.astype(MXU_DT), b.astype(MXU_DT), preferred_element_type=F32)


def _bmm_nt(a, b):
    return jnp.einsum('bik,bjk->bij', a.astype(MXU_DT), b.astype(MXU_DT), preferred_element_type=F32)


def _bmm_tn(a, b):
    return jnp.einsum('bki,bkj->bij', a.astype(MXU_DT), b.astype(MXU_DT), preferred_element_type=F32)


def _silu(x):
    return x * jax.nn.sigmoid(x)


def _adaln_kernel(c_ref, w_ref, b_ref, o_ref):
    o_ref[0] = _mm(_silu(c_ref[...]), w_ref[0]) + b_ref[0]


def adaln_all(cond, ada_w, ada_b):
    depth, d, n = ada_w.shape
    tn = 1536
    return pl.pallas_call(
        _adaln_kernel,
        grid=(depth, n // tn),
        in_specs=[pl.BlockSpec(cond.shape, lambda l, j: (0, 0)),
                  pl.BlockSpec((1, d, tn), lambda l, j: (l, 0, j)),
                  pl.BlockSpec((1, 1, tn), lambda l, j: (l, 0, j))],
        out_specs=pl.BlockSpec((1, cond.shape[0], tn), lambda l, j: (l, 0, j)),
        out_shape=jax.ShapeDtypeStruct((depth, cond.shape[0], n), F32),
        compiler_params=_cp(("arbitrary", "arbitrary")),
        name="adaln",
    )(cond, ada_w, ada_b.reshape(depth, 1, n))


def _norm_mod(x, gain, shift, scale):
    y = x * lax.rsqrt(jnp.mean(x * x, axis=-1, keepdims=True) + EPS)
    return (y * gain) * (1.0 + scale) + shift


def _inproj_kernel(x_ref, mod_ref, g_ref, w_ref, o_ref, *, n_chunk):
    h = _norm_mod(x_ref[0], g_ref[...], mod_ref[0:1, :], mod_ref[1:2, :]).astype(MXU_DT)
    n = o_ref.shape[-1]
    for n0 in range(0, n, n_chunk):
        n1 = min(n0 + n_chunk, n)
        o_ref[0, :, n0:n1] = jnp.dot(h, w_ref[:, n0:n1], preferred_element_type=F32)


def in_projection(x, mod, gain, w, n_chunk):
    b, t, d = x.shape
    n = w.shape[1]
    return pl.pallas_call(
        functools.partial(_inproj_kernel, n_chunk=n_chunk),
        grid=(b, t // TM),
        in_specs=[pl.BlockSpec((1, TM, d), lambda i, j: (i, j, 0)),
                  pl.BlockSpec((None, None, 6, d), lambda i, j: (i, jnp.minimum(j, 1), 0, 0)),
                  pl.BlockSpec((1, d), lambda i, j: (0, 0)),
                  pl.BlockSpec((d, n), lambda i, j: (0, 0))],
        out_specs=pl.BlockSpec((1, TM, n), lambda i, j: (i, j, 0)),
        out_shape=jax.ShapeDtypeStruct((b, t, n), F32),
        compiler_params=_cp(("parallel", "arbitrary")),
        name="in_projection",
    )(x, mod, gain.reshape(1, d), w)


def _na_block_start(j):
    return jnp.clip(NA_RB * (j - 1) - NA_WIN_R // 2, 0, GRID_W - NA_KROWS)


def na_bias_tables(rpb):
    rows = GRID_W
    n_dr = 2 * NA_WIN_R - 1
    cq = np.arange(GRID_W)[:, None]
    ck = np.arange(GRID_W)[None, :]
    c0 = np.clip(cq - NA_WIN_C // 2, 0, GRID_W - NA_WIN_C)
    ok_c = (ck >= c0) & (ck < c0 + NA_WIN_C)
    ic = np.clip(ck - cq + NA_WIN_C - 1, 0, 2 * NA_WIN_C - 2)
    tiles = jnp.where(ok_c[None, None], rpb.astype(F32)[:, :, ic], NEG)
    tiles = jnp.concatenate([tiles, jnp.full_like(tiles[:, :1], NEG)], axis=1)
    idx = np.full((4, NA_RB, NA_KROWS), n_dr, np.int32)
    for var, rb in enumerate((0, 1, rows // NA_RB - 1)):
        start = int(np.clip(NA_RB * rb - NA_WIN_R // 2, 0, rows - NA_KROWS))
        for qi in range(NA_RB):
            rq = NA_RB * rb + qi
            r0 = int(np.clip(rq - NA_WIN_R // 2, 0, rows - NA_WIN_R))
            for kj in range(NA_KROWS):
                rk = start + kj
                if r0 <= rk < r0 + NA_WIN_R:
                    idx[var, qi, kj] = rk - rq + NA_WIN_R - 1
    full = tiles[:, idx]
    return full.transpose(0, 1, 2, 4, 3, 5).reshape(rpb.shape[0], 4, NA_QB, NA_KU)


def _na_kernel(q_ref, k_ref, v_ref, qg_ref, kg_ref, bias_ref, o_ref, kn_ref):
    j = pl.program_id(2)
    is_h0 = lax.broadcasted_iota(jnp.int32, (1, LANES), 1) < NA_DH

    def headnorm(x, gain):
        x2 = x * x
        s0 = jnp.sum(jnp.where(is_h0, x2, 0.0), axis=-1, keepdims=True)
        s1 = jnp.sum(jnp.where(is_h0, 0.0, x2), axis=-1, keepdims=True)
        ms = jnp.where(is_h0, s0, s1) * (1.0 / NA_DH)
        return (x * lax.rsqrt(ms + EPS)) * gain

    @pl.when(j == 0)
    def _():
        kn_ref[...] = headnorm(k_ref[0], kg_ref[...]).astype(MXU_DT)

    qn = headnorm(q_ref[0], qg_ref[...]) * (NA_DH ** -0.5)
    off = pl.multiple_of(CTX_LEN + _na_block_start(j) * GRID_W, GRID_W)
    k_loc = kn_ref[pl.ds(off, NA_KU), :]
    k_ctx = kn_ref[0:CTX_LEN, :]
    v_loc = v_ref[0, pl.ds(off, NA_KU), :].astype(MXU_DT)
    v_ctx = v_ref[0, 0:CTX_LEN, :].astype(MXU_DT)
    outs = []
    for h in range(2):
        qm = jnp.where(is_h0 if h == 0 else jnp.logical_not(is_h0), qn, 0.0)
        s_loc = _mm_nt(qm, k_loc) + bias_ref[h, 0]
        s_ctx = _mm_nt(qm, k_ctx)
        m = jnp.maximum(jnp.max(s_loc, axis=-1, keepdims=True), jnp.max(s_ctx, axis=-1, keepdims=True))
        p_loc = jnp.exp(s_loc - m)
        p_ctx = jnp.exp(s_ctx - m)
        den = jnp.sum(p_loc, axis=-1, keepdims=True) + jnp.sum(p_ctx, axis=-1, keepdims=True)
        outs.append((_mm(p_loc, v_loc) + _mm(p_ctx, v_ctx)) / den)
    o_ref[0] = jnp.where(is_h0, outs[0], outs[1])


def neighbourhood_attention(proj, q_gain, k_gain, bias):
    b, t, _ = proj.shape
    nq = t // NA_QB
    n_rb = GRID_W // NA_RB

    def bias_idx(i, hp, j):
        var = jnp.where(j == 0, 3, jnp.where(j == 1, 0, jnp.where(j == n_rb, 2, 1)))
        return (hp, var, 0, 0)

    tile2 = lambda g: jnp.tile(g.astype(F32), 2).reshape(1, LANES)
    return pl.pallas_call(
        _na_kernel,
        grid=(b, NA_HEADS // 2, nq),
        in_specs=[pl.BlockSpec((1, NA_QB, LANES), lambda i, hp, j: (i, j, E_Q // LANES + hp)),
                  pl.BlockSpec((1, t, LANES), lambda i, hp, j: (i, 0, E_K // LANES + hp)),
                  pl.BlockSpec((1, t, LANES), lambda i, hp, j: (i, 0, E_V // LANES + hp)),
                  pl.BlockSpec((1, LANES), lambda i, hp, j: (0, 0)),
                  pl.BlockSpec((1, LANES), lambda i, hp, j: (0, 0)),
                  pl.BlockSpec((2, 1, NA_QB, NA_KU), bias_idx)],
        out_specs=pl.BlockSpec((1, NA_QB, LANES), lambda i, hp, j: (i, j, hp)),
        out_shape=jax.ShapeDtypeStruct((b, t, NA_W), F32),
        scratch_shapes=[pltpu.VMEM((t, LANES), MXU_DT)],
        compiler_params=_cp(("parallel", "arbitrary", "arbitrary")),
        name="neighbourhood_attention",
    )(proj, proj, proj, tile2(q_gain), tile2(k_gain), bias)


def _scan_tile(i, n_tiles, reverse):
    if not reverse:
        return i
    return jnp.where(i == 0, 0, n_tiles - i)


def _tri_masks(n, reverse):
    r = lax.broadcasted_iota(jnp.int32, (n, n), 0)
    c = lax.broadcasted_iota(jnp.int32, (n, n), 1)
    if reverse:
        return c >= r, c > r, r >= c
    return c <= r, c < r, r <= c


def _cumsum_both(x_col, x_row, incl, incl_t):
    cum_col = jnp.sum(jnp.where(incl, x_row, 0.0), axis=1, keepdims=True)
    cum_row = jnp.sum(jnp.where(incl_t, x_col, 0.0), axis=0, keepdims=True)
    return cum_col, cum_row


def _rope(x, cos, sin):
    w = x.shape[-1]
    lane = lax.broadcasted_iota(jnp.int32, (1, w), 1)
    first = (lane % ML_DQK) < ML_DQK // 2
    swapped = jnp.where(first, pltpu.roll(x, w - ML_DQK // 2, 1), pltpu.roll(x, ML_DQK // 2, 1))
    return x * cos + swapped * sin


def _mlstm_kernel(q_ref, k_ref, v_ref, g_ref, gt_ref, gbr_ref, gbc_ref, cos_ref, sin_ref, o_ref,
                  c_ref, n_ref, m_ref, *, reverse):
    i = pl.program_id(1)
    d = 1 if reverse else 0

    @pl.when(i == 0)
    def _():
        c_ref[...] = jnp.zeros_like(c_ref)
        n_ref[...] = jnp.zeros_like(n_ref)
        m_ref[...] = jnp.zeros_like(m_ref)

    q = _rope(q_ref[0], cos_ref[...], sin_ref[...])
    k = _rope(k_ref[0] * (ML_DQK ** -0.5), cos_ref[...], sin_ref[...])
    g_col = g_ref[0] + gbr_ref[...]
    g_row = gt_ref[0] + gbc_ref[...]
    incl, _, incl_t = _tri_masks(ML_CHUNK, reverse)
    is_h0 = lax.broadcasted_iota(jnp.int32, (1, LANES), 1) < ML_DQK
    nc = TM // ML_CHUNK
    nh = ML_HEADS
    order = range(nc - 1, -1, -1) if reverse else range(nc)
    tokc = lambda ci: slice(ci * ML_CHUNK, (ci + 1) * ML_CHUNK)
    pair = lambda h: slice((h // 2) * LANES, (h // 2 + 1) * LANES)
    hmask = lambda h: is_h0 if h % 2 == 0 else jnp.logical_not(is_h0)
    stack = lambda fn: jnp.stack([fn(h, ci) for h in range(nh) for ci in range(nc)], axis=0)
    by_head = lambda a: a.reshape((nh, nc) + a.shape[1:])
    flat = lambda parts: jnp.stack(parts, axis=1).reshape((nh * nc,) + parts[0].shape[1:])
    qm = stack(lambda h, ci: jnp.where(hmask(h), q[tokc(ci), pair(h)], 0.0))
    km = stack(lambda h, ci: jnp.where(hmask(h), k[tokc(ci), pair(h)], 0.0))
    v = stack(lambda h, ci: v_ref[0, tokc(ci), h * ML_DV:(h + 1) * ML_DV])
    li_c = stack(lambda h, ci: g_col[tokc(ci), 8 * d + h:8 * d + h + 1])
    lf_c = jax.nn.log_sigmoid(stack(lambda h, ci: g_col[tokc(ci), 8 * d + 4 + h:8 * d + 5 + h]))
    li_r = stack(lambda h, ci: g_row[8 * d + h:8 * d + h + 1, tokc(ci)])
    lf_r = jax.nn.log_sigmoid(stack(lambda h, ci: g_row[8 * d + 4 + h:8 * d + 5 + h, tokc(ci)]))
    cum_c = jnp.sum(jnp.where(incl, lf_r, 0.0), axis=-1, keepdims=True)
    cum_r = jnp.sum(jnp.where(incl_t, lf_c, 0.0), axis=-2, keepdims=True)
    log_d = jnp.where(incl, cum_c - cum_r + li_r, NEG)
    m_loc = jnp.max(log_d, axis=-1, keepdims=True)
    cum_last = jnp.sum(lf_r, axis=-1, keepdims=True)
    log_w = cum_last - cum_c + li_c
    m_w = jnp.max(log_w, axis=-2, keepdims=True)
    cl4, mw4 = by_head(cum_last), by_head(m_w)
    m = m_ref[...][:, :, 0:1]
    m_prev, m_new = [None] * nc, [None] * nc
    for ci in order:
        m_prev[ci] = m
        m = jnp.maximum(cl4[:, ci] + m, mw4[:, ci])
        m_new[ci] = m
    m_ref[...] = jnp.broadcast_to(m, m_ref.shape)
    m_prev, m_new = flat(m_prev), flat(m_new)
    m_inter = cum_c + m_prev
    m_t = jnp.maximum(m_loc, m_inter)
    sg = _bmm_nt(qm, km) * jnp.exp(log_d - m_t)
    w_inter = jnp.exp(m_inter - m_t)
    kw = km * jnp.exp(log_w - m_new)
    upd_c, upd_n = by_head(_bmm_tn(kw, v)), by_head(jnp.sum(kw, axis=-2, keepdims=True))
    decay = by_head(jnp.exp(cum_last + m_prev - m_new))
    c_mat, n_vec = c_ref[...], n_ref[...]
    c_prev, n_prev = [None] * nc, [None] * nc
    for ci in order:
        c_prev[ci], n_prev[ci] = c_mat, n_vec
        c_mat = decay[:, ci] * c_mat + upd_c[:, ci]
        n_vec = decay[:, ci] * n_vec + upd_n[:, ci]
    c_ref[...] = c_mat
    n_ref[...] = n_vec
    num = _bmm(sg, v) + w_inter * _bmm(qm, flat(c_prev))
    den = jnp.sum(sg, axis=-1, keepdims=True) + w_inter * jnp.sum(qm * flat(n_prev), axis=-1, keepdims=True)
    h_out = by_head(num / jnp.maximum(jnp.abs(den), jnp.exp(-m_t)))
    for h in range(nh):
        for ci in range(nc):
            o_ref[0, 0, tokc(ci), h * ML_DV:(h + 1) * ML_DV] = h_out[h, ci]


def mlstm_direction(proj, gates_t, gate_b, cos, sin, reverse):
    b, t, _ = proj.shape
    n_tiles = t // TM
    tile = functools.partial(_scan_tile, n_tiles=n_tiles, reverse=reverse)
    gb = gate_b.astype(F32).reshape(16)
    gb_row = jnp.zeros((1, LANES), F32).at[0, :16].set(gb)
    gb_col = gb.reshape(16, 1)
    return pl.pallas_call(
        functools.partial(_mlstm_kernel, reverse=reverse),
        grid=(b, n_tiles),
        in_specs=[pl.BlockSpec((1, TM, ML_QK_W), lambda bi, i: (bi, tile(i), E_MQ // ML_QK_W)),
                  pl.BlockSpec((1, TM, ML_QK_W), lambda bi, i: (bi, tile(i), E_MK // ML_QK_W)),
                  pl.BlockSpec((1, TM, ML_V_W), lambda bi, i: (bi, tile(i), E_MV // ML_V_W)),
                  pl.BlockSpec((1, TM, LANES), lambda bi, i: (bi, tile(i), E_MG // LANES)),
                  pl.BlockSpec((1, 16, TM), lambda bi, i: (bi, 0, tile(i))),
                  pl.BlockSpec((1, LANES), lambda bi, i: (0, 0)),
                  pl.BlockSpec((16, 1), lambda bi, i: (0, 0)),
                  pl.BlockSpec((TM, ML_QK_W), lambda bi, i: (tile(i), 0)),
                  pl.BlockSpec((TM, ML_QK_W), lambda bi, i: (tile(i), 0))],
        out_specs=pl.BlockSpec((1, 1, TM, ML_V_W), lambda bi, i: (0, bi, tile(i), 0)),
        out_shape=jax.ShapeDtypeStruct((1, b, t, ML_V_W), F32),
        scratch_shapes=[pltpu.VMEM((ML_HEADS, LANES, ML_DV), F32),
                        pltpu.VMEM((ML_HEADS, 1, LANES), F32),
                        pltpu.VMEM((ML_HEADS, 1, LANES), F32)],
        compiler_params=_cp(("parallel", "arbitrary")),
        name="mlstm_bwd" if reverse else "mlstm_fwd",
    )(proj, proj, proj, proj, gates_t, gb_row, gb_col, cos, sin)[0]


def rope_tables(t):
    s = t - CTX_LEN
    tok = jnp.arange(s)
    n_freq = ML_DQK // 4
    inv = ROPE_THETA ** (-jnp.arange(n_freq, dtype=F32) / n_freq)
    ang = jnp.concatenate([(tok // GRID_W).astype(F32)[:, None] * inv,
                           (tok % GRID_W).astype(F32)[:, None] * inv], axis=-1)
    cos, sin = jnp.cos(ang), jnp.sin(ang)
    cos_h = jnp.concatenate([cos, cos], axis=-1)
    sin_h = jnp.concatenate([-sin, sin], axis=-1)
    cos_f = jnp.concatenate([jnp.ones((CTX_LEN, ML_DQK), F32), cos_h], axis=0)
    sin_f = jnp.concatenate([jnp.zeros((CTX_LEN, ML_DQK), F32), sin_h], axis=0)
    return jnp.tile(cos_f, (1, ML_HEADS)), jnp.tile(sin_f, (1, ML_HEADS))


def _dnconv_kernel(x_ref, w_ref, o_ref):
    cb = pl.program_id(1)
    x = x_ref[0]
    t = x.shape[0]
    tok = lax.broadcasted_iota(jnp.int32, (t, 1), 0)
    seg_lo = jnp.where(tok < CTX_LEN, 0, CTX_LEN)
    seg_hi = jnp.where(tok < CTX_LEN, CTX_LEN, t)
    acc = jnp.zeros_like(x)
    for j in range(DN_CONV):
        o = j - DN_CONV // 2
        xs = x if o == 0 else pltpu.roll(x, (-o) % t, 0)
        ok = (tok + o >= seg_lo) & (tok + o < seg_hi)
        acc = acc + jnp.where(ok, xs, 0.0) * w_ref[j:j + 1, :]
    y = _silu(acc)
    nrm = lax.rsqrt(jnp.sum(y * y, axis=-1, keepdims=True) + EPS)
    q_heads = DN_QK_W // LANES
    o_ref[0] = jnp.where(cb < q_heads, (y * nrm) * (DN_DK ** -0.5), jnp.where(cb < 2 * q_heads, y * nrm, y))


def dn_conv(proj, conv_w):
    b, t, _ = proj.shape
    return pl.pallas_call(
        _dnconv_kernel,
        grid=(b, DN_CONV_CH // LANES),
        in_specs=[pl.BlockSpec((1, t, LANES), lambda i, c: (i, 0, c)),
                  pl.BlockSpec((DN_CONV, LANES), lambda i, c: (0, c))],
        out_specs=pl.BlockSpec((1, t, LANES), lambda i, c: (i, 0, c)),
        out_shape=jax.ShapeDtypeStruct((b, t, DN_CONV_CH), F32),
        compiler_params=_cp(("parallel", "arbitrary")),
        name="dn_conv",
    )(proj, conv_w.astype(F32).T)


def _gdn_kernel(q_ref, k_ref, v_ref, ab_ref, at_ref, par_ref, parc_ref, o_ref, s_ref, *, reverse):
    i = pl.program_id(1)
    d = 1 if reverse else 0

    @pl.when(i == 0)
    def _():
        s_ref[...] = jnp.zeros_like(s_ref)

    incl, strict, incl_t = _tri_masks(DN_CHUNK, reverse)
    r = lax.broadcasted_iota(jnp.int32, (DN_CHUNK, DN_CHUNK), 0)
    c = lax.broadcasted_iota(jnp.int32, (DN_CHUNK, DN_CHUNK), 1)
    eye = (r == c).astype(F32)
    ab = ab_ref[0]
    g_all = -jnp.exp(par_ref[0:1, :]) * jax.nn.softplus(ab + par_ref[1:2, :])
    beta_full = jax.nn.sigmoid(ab)
    gt_all = -jnp.exp(parc_ref[:, 0:1]) * jax.nn.softplus(at_ref[0] + parc_ref[:, 1:2])

    nc = TM // DN_CHUNK
    nh = DN_HEADS
    tokc = lambda ci: slice(ci * DN_CHUNK, (ci + 1) * DN_CHUNK)
    hl = lambda h: slice(h * LANES, (h + 1) * LANES)
    stack = lambda fn: jnp.stack([fn(h, ci) for h in range(nh) for ci in range(nc)], axis=0)
    q = stack(lambda h, ci: q_ref[0, tokc(ci), hl(h)])
    k = stack(lambda h, ci: k_ref[0, tokc(ci), hl(h)])
    v = stack(lambda h, ci: v_ref[0, tokc(ci), hl(h)])
    g_c = stack(lambda h, ci: g_all[tokc(ci), 8 * d + h:8 * d + h + 1])
    beta = stack(lambda h, ci: beta_full[tokc(ci), 16 + 8 * d + h:17 + 8 * d + h])
    g_r = stack(lambda h, ci: gt_all[8 * d + h:8 * d + h + 1, tokc(ci)])
    gc_c = jnp.sum(jnp.where(incl, g_r, 0.0), axis=-1, keepdims=True)
    gc_r = jnp.sum(jnp.where(incl_t, g_c, 0.0), axis=-2, keepdims=True)
    decay = jnp.exp(jnp.where(incl, gc_c - gc_r, NEG))
    kb = k * beta
    a_mat = jnp.where(strict, _bmm_nt(kb, k) * decay, 0.0)
    pw = -a_mat
    t_inv = eye + pw
    for _ in range(5):
        pw = _bmm(pw, pw)
        t_inv = t_inv + _bmm(t_inv, pw)
    egc = jnp.exp(gc_c)
    uw = _bmm(t_inv, jnp.concatenate([v * beta, kb * egc], axis=-1))
    attn = _bmm_nt(q, k) * decay
    g_last = jnp.sum(g_c, axis=-2, keepdims=True)
    kg = k * jnp.exp(g_last - gc_c)
    ng = _bmm_tn(kg, uw)
    au = _bmm(attn, uw)
    lhs = jnp.concatenate([ng[:, :, DN_DV:], q * egc - au[:, :, DN_DV:]], axis=1)
    by_head = lambda a: a.reshape((nh, nc) + a.shape[1:])
    lhs, n_mat, o2, dec = by_head(lhs), by_head(ng[:, :, :DN_DV]), by_head(au[:, :, :DN_DV]), by_head(jnp.exp(g_last))
    s_mat = s_ref[...]
    for ci in (range(nc - 1, -1, -1) if reverse else range(nc)):
        prod = _bmm(lhs[:, ci], s_mat)
        o = prod[:, DN_DK:] + o2[:, ci]
        for h in range(nh):
            o_ref[0, 0, tokc(ci), hl(h)] = o[h]
        s_mat = s_mat * dec[:, ci] + n_mat[:, ci] - prod[:, :DN_DK]
    s_ref[...] = s_mat


def gdn_direction(qkv, proj, gates_t, a_log, dt_bias, reverse):
    b, t, _ = qkv.shape
    n_tiles = t // TM
    tile = functools.partial(_scan_tile, n_tiles=n_tiles, reverse=reverse)
    par = jnp.zeros((2, LANES), F32)
    par = par.at[0, :16].set(a_log.astype(F32).reshape(16)).at[1, :16].set(dt_bias.astype(F32).reshape(16))
    parc = par[:, :16].T
    return pl.pallas_call(
        functools.partial(_gdn_kernel, reverse=reverse),
        grid=(b, n_tiles),
        in_specs=[pl.BlockSpec((1, TM, DN_QK_W), lambda bi, i: (bi, tile(i), 0)),
                  pl.BlockSpec((1, TM, DN_QK_W), lambda bi, i: (bi, tile(i), 1)),
                  pl.BlockSpec((1, TM, DN_V_W), lambda bi, i: (bi, tile(i), 2)),
                  pl.BlockSpec((1, TM, LANES), lambda bi, i: (bi, tile(i), O_A // LANES)),
                  pl.BlockSpec((1, 16, TM), lambda bi, i: (bi, 0, tile(i))),
                  pl.BlockSpec((2, LANES), lambda bi, i: (0, 0)),
                  pl.BlockSpec((16, 2), lambda bi, i: (0, 0))],
        out_specs=pl.BlockSpec((1, 1, TM, DN_V_W), lambda bi, i: (0, bi, tile(i), 0)),
        out_shape=jax.ShapeDtypeStruct((1, b, t, DN_V_W), F32),
        scratch_shapes=[pltpu.VMEM((DN_HEADS, DN_DK, DN_DV), F32)],
        compiler_params=_cp(("parallel", "arbitrary")),
        name="gdn_bwd" if reverse else "gdn_fwd",
    )(qkv, qkv, qkv, proj, gates_t, par, parc)[0]


def _headwise_rms(x, gain, width):
    parts = []
    for h0 in range(0, x.shape[-1], width):
        xh = x[:, h0:h0 + width]
        parts.append((xh * lax.rsqrt(jnp.mean(xh * xh, axis=-1, keepdims=True) + EPS)) * gain)
    return parts


def _outproj_kernel(*refs, even):
    if even:
        (a_ref, hf_ref, hb_ref, og_ref, gain_ref, x_ref, mod_ref, g2_ref, w_ref, rw_ref, rb_ref,
         xo_ref, hm_ref, tv_ref, ti_ref) = refs
        hs = _headwise_rms(hf_ref[0] + hb_ref[0], gain_ref[...], ML_DV)
        og = og_ref[0]
        mix = [a_ref[0]] + [hs[h] * jax.nn.sigmoid(og[:, h * ML_DV:(h + 1) * ML_DV]) for h in range(ML_HEADS)]
    else:
        (of_ref, ob_ref, z_ref, gain_ref, x_ref, mod_ref, g2_ref, w_ref, rw_ref, rb_ref,
         xo_ref, hm_ref, tv_ref, ti_ref) = refs
        hs = _headwise_rms(of_ref[0] + ob_ref[0], gain_ref[...], DN_DV)
        z = z_ref[0]
        mix = [hs[h] * _silu(z[:, h * DN_DV:(h + 1) * DN_DV]) for h in range(DN_HEADS)]
    mix = jnp.concatenate(mix, axis=-1)
    x_new = x_ref[0] + mod_ref[2:3, :] * _mm(mix, w_ref[...])
    xo_ref[0] = x_new
    hm = _norm_mod(x_new, g2_ref[...], mod_ref[3:4, :], mod_ref[4:5, :])
    hm_ref[0] = hm.astype(hm_ref.dtype)
    rw = rw_ref[...]
    hm_hi, rw_hi = hm.astype(MXU_DT), rw.astype(MXU_DT)
    hm_lo, rw_lo = hm - hm_hi.astype(F32), rw - rw_hi.astype(F32)
    logits = (_mm(hm_hi, rw_hi) + (_mm(hm_hi, rw_lo) + _mm(hm_lo, rw_hi))) + rb_ref[...]
    lane = lax.broadcasted_iota(jnp.int32, logits.shape, 1)
    vals = jnp.zeros_like(logits)
    idxs = jnp.zeros(logits.shape, jnp.int32)
    work = jnp.where(lane < N_EXPERTS, logits, -jnp.inf)
    for kk in range(TOP_K):
        mx = jnp.max(work, axis=-1, keepdims=True)
        am = jnp.min(jnp.where(work == mx, lane, LANES), axis=-1, keepdims=True)
        vals = jnp.where(lane == kk, mx, vals)
        idxs = jnp.where(lane == kk, am, idxs)
        work = jnp.where(lane == am, -jnp.inf, work)
    ex = jnp.where(lane < TOP_K, jnp.exp(vals - jnp.max(jnp.where(lane < TOP_K, vals, -jnp.inf), axis=-1, keepdims=True)), 0.0)
    tv_ref[0] = ex / jnp.sum(ex, axis=-1, keepdims=True)
    ti_ref[0] = idxs


def out_projection(even, mixer_in, gain, x, mod, g2, w_out, router_w, router_b):
    b, t, d = x.shape
    tok = lambda wdt, blk: pl.BlockSpec((1, TM, wdt), lambda i, j: (i, j, blk))
    if even:
        a, hf, hb, proj = mixer_in
        ins = [a, hf, hb, proj]
        specs = [tok(NA_W, 0), tok(ML_V_W, 0), tok(ML_V_W, 0), tok(ML_V_W, E_MO // ML_V_W)]
    else:
        of, ob, proj = mixer_in
        ins = [of, ob, proj]
        specs = [tok(DN_V_W, 0), tok(DN_V_W, 0), tok(DN_V_W, O_Z // DN_V_W)]
    rw = jnp.zeros((d, LANES), F32).at[:, :N_EXPERTS].set(router_w.astype(F32))
    rb = jnp.zeros((1, LANES), F32).at[0, :N_EXPERTS].set(router_b.astype(F32))
    const = lambda shape: pl.BlockSpec(shape, lambda i, j: (0,) * len(shape))
    ins += [gain.astype(F32).reshape(1, LANES), x, mod, g2.reshape(1, d), w_out.astype(MXU_DT), rw, rb]
    specs += [const((1, LANES)), tok(d, 0),
              pl.BlockSpec((None, None, 6, d), lambda i, j: (i, jnp.minimum(j, 1), 0, 0)),
              const((1, d)), const(w_out.shape), const((d, LANES)), const((1, LANES))]
    return pl.pallas_call(
        functools.partial(_outproj_kernel, even=even),
        grid=(b, t // TM),
        in_specs=specs,
        out_specs=[tok(d, 0), tok(d, 0), tok(LANES, 0), tok(LANES, 0)],
        out_shape=[jax.ShapeDtypeStruct((b, t, d), F32), jax.ShapeDtypeStruct((b, t, d), F32),
                   jax.ShapeDtypeStruct((b, t, LANES), F32), jax.ShapeDtypeStruct((b, t, LANES), jnp.int32)],
        compiler_params=_cp(("parallel", "arbitrary")),
        name="out_projection_even" if even else "out_projection_odd",
    )(*ins)


MOE_GRP = 2 * LANES


def _swiglu_perm():
    p = np.zeros((MOE_GRP, MOE_GRP), np.float32)
    j = np.arange(LANES)
    p[2 * j, j] = 1.0
    p[2 * j + 1, LANES + j] = 1.0
    return p


def _swiglu_col_order(n):
    g = np.arange(n // MOE_GRP)[:, None] * MOE_GRP
    j = np.arange(LANES)[None, :]
    return np.concatenate([g + 2 * j, g + 2 * j + 1], axis=1).reshape(-1)


def _experts_kernel(be_ref, nb_ref, x_ref, w1_ref, b1_ref, w2_ref, b2_ref, perm_ref, o_ref, w1s_ref, w2s_ref):
    i = pl.program_id(0)
    n_grp = w1s_ref.shape[1] // MOE_GRP

    @pl.when((i == 0) | (be_ref[i] != be_ref[jnp.maximum(i - 1, 0)]))
    def _():
        for g in range(n_grp):
            cols = slice(g * MOE_GRP, (g + 1) * MOE_GRP)
            w1s_ref[:, cols] = jnp.dot(w1_ref[0, :, cols].astype(MXU_DT), perm_ref[...],
                                       preferred_element_type=F32).astype(MXU_DT)
        w2s_ref[...] = w2_ref[0].astype(MXU_DT)

    @pl.when(i < nb_ref[0])
    def _():
        x = x_ref[...].astype(MXU_DT)
        acts = []
        for g in range(n_grp):
            cols = slice(g * MOE_GRP, (g + 1) * MOE_GRP)
            hid = jnp.dot(x, w1s_ref[:, cols], preferred_element_type=F32) + b1_ref[0, :, cols]
            glu = jnp.minimum(hid[:, :LANES], SWIGLU_LIMIT)
            lin = jnp.clip(hid[:, LANES:], -SWIGLU_LIMIT, SWIGLU_LIMIT)
            acts.append((glu * jax.nn.sigmoid(SWIGLU_ALPHA * glu) * (lin + 1.0)).astype(MXU_DT))
        act = jnp.concatenate(acts, axis=-1)
        o_ref[...] = (jnp.dot(act, w2s_ref[...], preferred_element_type=F32) + b2_ref[0]).astype(o_ref.dtype)

    @pl.when(i >= nb_ref[0])
    def _():
        o_ref[...] = jnp.zeros_like(o_ref)


def experts_ffn(xs, block_exp, n_used, layer, w1, b1p, w2, b2):
    n_slots, d = xs.shape
    n_blocks = n_slots // MOE_BM
    dh2 = w1.shape[-1]
    wspec = lambda shape: pl.BlockSpec((None, 1) + shape, lambda i, be, nb: (layer, be[i], 0, 0))
    bspec = lambda shape: pl.BlockSpec((1,) + shape, lambda i, be, nb: (be[i], 0, 0))
    return pl.pallas_call(
        _experts_kernel,
        grid_spec=pltpu.PrefetchScalarGridSpec(
            num_scalar_prefetch=2,
            grid=(n_blocks,),
            in_specs=[pl.BlockSpec((MOE_BM, d), lambda i, be, nb: (jnp.minimum(i, nb[0] - 1), 0)),
                      wspec((d, dh2)), bspec((1, dh2)), wspec((dh2 // 2, d)), bspec((1, d)),
                      pl.BlockSpec((MOE_GRP, MOE_GRP), lambda i, be, nb: (0, 0))],
            out_specs=pl.BlockSpec((MOE_BM, d), lambda i, be, nb: (i, 0)),
            scratch_shapes=[pltpu.VMEM((d, dh2), MXU_DT), pltpu.VMEM((dh2 // 2, d), MXU_DT)],
        ),
        out_shape=jax.ShapeDtypeStruct((n_slots, d), MXU_DT),
        compiler_params=_cp(("arbitrary",)),
        name="experts_ffn",
    )(block_exp, n_used, xs, w1, b1p, w2, b2, jnp.asarray(_swiglu_perm(), MXU_DT))


def _combine_kernel(x_ref, y_ref, gate_ref, mod_ref, o_ref):
    gate = gate_ref[0]
    f = y_ref[0, 0].astype(F32) * gate[:, 0:1]
    for kk in range(1, TOP_K):
        f = f + y_ref[0, kk].astype(F32) * gate[:, kk:kk + 1]
    o_ref[0] = x_ref[0] + mod_ref[5:6, :] * f


def moe_combine(x, yg, gate, mod):
    b, t, d = x.shape
    return pl.pallas_call(
        _combine_kernel,
        grid=(b, t // TM),
        in_specs=[pl.BlockSpec((1, TM, d), lambda i, j: (i, j, 0)),
                  pl.BlockSpec((1, TOP_K, TM, d), lambda i, j: (i, 0, j, 0)),
                  pl.BlockSpec((1, TM, LANES), lambda i, j: (i, j, 0)),
                  pl.BlockSpec((None, None, 6, d), lambda i, j: (i, jnp.minimum(j, 1), 0, 0))],
        out_specs=pl.BlockSpec((1, TM, d), lambda i, j: (i, j, 0)),
        out_shape=jax.ShapeDtypeStruct((b, t, d), F32),
        compiler_params=_cp(("parallel", "arbitrary")),
        name="moe_combine",
    )(x, yg, gate, mod)


def moe_layer(x_new, hm, gate, top_idx, mod, layer, w1, b1, w2, b2):
    b, t, d = x_new.shape
    n_tok = b * t
    n_asg = n_tok * TOP_K
    flat_exp = top_idx[:, :, :TOP_K].transpose(0, 2, 1).reshape(n_asg)
    cb = 256
    onehot = (flat_exp[:, None] == jnp.arange(N_EXPERTS, dtype=jnp.int32)[None, :]).astype(F32)
    onehot = onehot.reshape(n_asg // cb, cb, N_EXPERTS)
    within = jnp.einsum('ij,bjk->bik', jnp.tril(jnp.ones((cb, cb), F32)), onehot)
    blk_tot = within[:, -1, :]
    blk_off = jnp.cumsum(blk_tot, axis=0) - blk_tot
    counts = (blk_off[-1] + blk_tot[-1]).astype(jnp.int32)
    rank = jnp.sum((within - onehot + blk_off[:, None, :]) * onehot, axis=-1).reshape(n_asg).astype(jnp.int32)
    padded = (counts + MOE_BM - 1) // MOE_BM * MOE_BM
    pad_end = jnp.cumsum(padded)
    dest = (pad_end - padded)[flat_exp] + rank
    n_blocks = -(-n_asg // MOE_BM) + N_EXPERTS
    n_slots = n_blocks * MOE_BM
    block_start = jnp.arange(n_blocks, dtype=jnp.int32) * MOE_BM
    block_exp = jnp.minimum(jnp.sum((pad_end[None, :] <= block_start[:, None]).astype(jnp.int32), axis=1),
                            N_EXPERTS - 1)
    n_used = (pad_end[-1] // MOE_BM).astype(jnp.int32).reshape(1)
    asg_bits = (n_asg - 1).bit_length()
    assert N_EXPERTS << asg_bits < 2 ** 31
    sorted_asg = jnp.sort((flat_exp << asg_bits) + jnp.arange(n_asg, dtype=jnp.int32)) & ((1 << asg_bits) - 1)
    first_asg = jnp.cumsum(counts) - counts
    in_exp = (block_start - (pad_end - padded)[block_exp])[:, None] + jnp.arange(MOE_BM, dtype=jnp.int32)[None, :]
    is_row = in_exp < counts[block_exp][:, None]
    asg = sorted_asg[jnp.clip(first_asg[block_exp][:, None] + in_exp, 0, n_asg - 1).reshape(n_slots)]
    tok = (asg // (TOP_K * t)) * t + asg % t
    slot_tok = jnp.where(is_row.reshape(n_slots), tok, jnp.arange(n_slots, dtype=jnp.int32) % n_tok)
    xs = hm.reshape(n_tok, d)[slot_tok]
    b1p = b1.astype(F32)[:, _swiglu_col_order(b1.shape[-1])][:, None, :]
    y = experts_ffn(xs, block_exp, n_used, layer, w1, b1p, w2, b2[:, None, :].astype(F32))
    yg = y[dest].reshape(b, TOP_K, t, d)
    return moe_combine(x_new, yg, gate, mod)


def _pad_cols(w, n):
    return jnp.pad(w, ((0, 0), (0, n - w.shape[1]))).astype(MXU_DT)


N_STREAMS = 2


def kernel(x, c, ctx, c_ctx, ada_w, ada_b, norm1_g, norm2_g, ev_w_in, ev_w_out, na_q_gain, na_k_gain, na_rpb,
           ml_gate_b, ml_out_gain, od_w_in, od_w_out, dn_conv_w, dn_a_log, dn_dt_bias, dn_out_gain, router_w,
           router_b, exp_w1, exp_b1, exp_w2, exp_b2):
    b = x.shape[0]
    assert b % N_STREAMS == 0
    g = b // N_STREAMS
    outs = [_trunk(x[i:i + g], c[i:i + g], ctx[i:i + g], c_ctx, ada_w, ada_b, norm1_g, norm2_g, ev_w_in, ev_w_out,
                   na_q_gain, na_k_gain, na_rpb, ml_gate_b, ml_out_gain, od_w_in, od_w_out, dn_conv_w, dn_a_log,
                   dn_dt_bias, dn_out_gain, router_w, router_b, exp_w1, exp_b1, exp_w2, exp_b2)
            for i in range(0, b, g)]
    return jnp.concatenate(outs, axis=0)


def _trunk(x, c, ctx, c_ctx, ada_w, ada_b, norm1_g, norm2_g, ev_w_in, ev_w_out, na_q_gain, na_k_gain, na_rpb,
           ml_gate_b, ml_out_gain, od_w_in, od_w_out, dn_conv_w, dn_a_log, dn_dt_bias, dn_out_gain, router_w,
           router_b, exp_w1, exp_b1, exp_w2, exp_b2):
    b, s, d = x.shape
    depth = ada_w.shape[0]
    t = CTX_LEN + s
    xs = jnp.concatenate([ctx, x], axis=1)
    cond = jnp.zeros((16, d), F32).at[:b].set(c).at[b].set(c_ctx)
    mods = adaln_all(cond, ada_w, ada_b)
    mod_lat = mods[:, :b].reshape(depth, b, 1, 6, d)
    mod_ctx = jnp.broadcast_to(mods[:, b].reshape(depth, 1, 1, 6, d), (depth, b, 1, 6, d))
    mod_all = jnp.concatenate([mod_ctx, mod_lat], axis=2)
    cos, sin = rope_tables(t)
    for layer in range(depth):
        mod = mod_all[layer]
        if layer % 2 == 0:
            e = layer // 2
            proj = in_projection(xs, mod, norm1_g[layer], _pad_cols(ev_w_in[e], EVEN_PAD), INPROJ_CHUNK)
            a = neighbourhood_attention(proj, na_q_gain[e], na_k_gain[e], na_bias_tables(na_rpb[e]))
            gates_t = proj[:, :, E_MG:E_MG + 16].transpose(0, 2, 1)
            hf = mlstm_direction(proj, gates_t, ml_gate_b[e], cos, sin, False)
            hb = mlstm_direction(proj, gates_t, ml_gate_b[e], cos, sin, True)
            mixer_in, gain, w_out = (a, hf, hb, proj), ml_out_gain[e], ev_w_out[e]
        else:
            o = layer // 2
            proj = in_projection(xs, mod, norm1_g[layer], _pad_cols(od_w_in[o], ODD_PAD), INPROJ_CHUNK)
            qkv = dn_conv(proj, dn_conv_w[o])
            gates_t = proj[:, :, O_A:O_A + 16].transpose(0, 2, 1)
            of = gdn_direction(qkv, proj, gates_t, dn_a_log[o], dn_dt_bias[o], False)
            ob = gdn_direction(qkv, proj, gates_t, dn_a_log[o], dn_dt_bias[o], True)
            mixer_in, gain, w_out = (of, ob, proj), dn_out_gain[o], od_w_out[o]
        x_new, hm, gate, top_idx = out_projection(layer % 2 == 0, mixer_in, gain, xs, mod, norm2_g[layer], w_out,
                                                  router_w[layer], router_b[layer])
        xs = moe_layer(x_new, hm, gate, top_idx, mod, layer, exp_w1, exp_b1[layer], exp_w2, exp_b2[layer])
    return xs[:, CTX_LEN:]
```

```python
import functools

import numpy as np
import jax
import jax.numpy as jnp
from jax import lax
from jax.experimental import pallas as pl
from jax.experimental.pallas import tpu as pltpu

F32 = jnp.float32
MXU_DT = jnp.bfloat16
EPS = 1e-6
NEG = -1e30

D_MODEL = 1024
GRID_W = 64
CTX_LEN = 256
TM = 256
LANES = 128
MXU_N = 256
INPROJ_CHUNK = 4 * MXU_N

NA_HEADS, NA_DH = 8, 64
NA_W = NA_HEADS * NA_DH
NA_WIN_R, NA_WIN_C = 8, 16
NA_RB = 4
NA_QB = NA_RB * GRID_W
NA_KROWS = NA_RB + NA_WIN_R - 1
NA_KU = NA_KROWS * GRID_W

ML_HEADS, ML_DQK, ML_DV, ML_CHUNK = 4, 64, 128, 64
ML_QK_W, ML_V_W = ML_HEADS * ML_DQK, ML_HEADS * ML_DV
ROPE_THETA = 10000.0
EVEN_IN = 3 * NA_W + 2 * ML_QK_W + 2 * ML_V_W + 4 * ML_HEADS
EVEN_PAD = 3200
E_Q, E_K, E_V = 0, NA_W, 2 * NA_W
E_MQ = 3 * NA_W
E_MK = E_MQ + ML_QK_W
E_MV = E_MK + ML_QK_W
E_MO = E_MV + ML_V_W
E_MG = E_MO + ML_V_W

DN_HEADS, DN_DK, DN_DV, DN_CHUNK, DN_CONV = 8, 128, 128, 64, 5
DN_QK_W, DN_V_W = DN_HEADS * DN_DK, DN_HEADS * DN_DV
DN_CONV_CH = 2 * DN_QK_W + DN_V_W
ODD_IN = DN_CONV_CH + DN_V_W + 4 * DN_HEADS
ODD_PAD = 4224
O_Z = DN_CONV_CH
O_A = O_Z + DN_V_W

N_EXPERTS, TOP_K, D_EXPERT = 32, 4, 1024
SWIGLU_LIMIT, SWIGLU_ALPHA = 7.0, 1.702
MOE_BM = 256

VMEM_LIMIT = 56 * 1024 * 1024


def _cp(sem):
    return pltpu.CompilerParams(dimension_semantics=sem, vmem_limit_bytes=VMEM_LIMIT)


def _mm(a, b):
    return jnp.dot(a.astype(MXU_DT), b.astype(MXU_DT), preferred_element_type=F32)


def _mm_nt(a, b):
    return lax.dot_general(a.astype(MXU_DT), b.astype(MXU_DT), (((1,), (1,)), ((), ())),
                           preferred_element_type=F32)


def _mm_tn(a, b):
    return lax.dot_general(a.astype(MXU_DT), b.astype(MXU_DT), (((0,), (0,)), ((), ())),
                           preferred_element_type=F32)


def _bmm(a, b):
    return jnp.einsum('bij,bjk->bik', a.astype(MXU_DT), b.astype(MXU_DT), preferred_element_type=F32)


def _bmm_nt(a, b):
    return jnp.einsum('bik,bjk->bij', a.astype(MXU_DT), b.astype(MXU_DT), preferred_element_type=F32)


def _bmm_tn(a, b):
    return jnp.einsum('bki,bkj->bij', a.astype(MXU_DT), b.astype(MXU_DT), preferred_element_type=F32)


def _silu(x):
    return x * jax.nn.sigmoid(x)


def _adaln_kernel(c_ref, w_ref, b_ref, o_ref):
    o_ref[0] = _mm(_silu(c_ref[...]), w_ref[0]) + b_ref[0]


def adaln_all(cond, ada_w, ada_b):
    depth, d, n = ada_w.shape
    tn = 1536
    return pl.pallas_call(
        _adaln_kernel,
        grid=(depth, n // tn),
        in_specs=[pl.BlockSpec(cond.shape, lambda l, j: (0, 0)),
                  pl.BlockSpec((1, d, tn), lambda l, j: (l, 0, j)),
                  pl.BlockSpec((1, 1, tn), lambda l, j: (l, 0, j))],
        out_specs=pl.BlockSpec((1, cond.shape[0], tn), lambda l, j: (l, 0, j)),
        out_shape=jax.ShapeDtypeStruct((depth, cond.shape[0], n), F32),
        compiler_params=_cp(("arbitrary", "arbitrary")),
        name="adaln",
    )(cond, ada_w, ada_b.reshape(depth, 1, n))


def _norm_mod(x, gain, shift, scale):
    y = x * lax.rsqrt(jnp.mean(x * x, axis=-1, keepdims=True) + EPS)
    return (y * gain) * (1.0 + scale) + shift


def _inproj_kernel(x_ref, mod_ref, g_ref, w_ref, o_ref, *, n_chunk):
    h = _norm_mod(x_ref[0], g_ref[...], mod_ref[0:1, :], mod_ref[1:2, :]).astype(MXU_DT)
    n = o_ref.shape[-1]
    for n0 in range(0, n, n_chunk):
        n1 = min(n0 + n_chunk, n)
        o_ref[0, :, n0:n1] = jnp.dot(h, w_ref[:, n0:n1], preferred_element_type=F32)


def in_projection(x, mod, gain, w, n_chunk):
    b, t, d = x.shape
    n = w.shape[1]
    return pl.pallas_call(
        functools.partial(_inproj_kernel, n_chunk=n_chunk),
        grid=(b, t // TM),
        in_specs=[pl.BlockSpec((1, TM, d), lambda i, j: (i, j, 0)),
                  pl.BlockSpec((None, None, 6, d), lambda i, j: (i, jnp.minimum(j, 1), 0, 0)),
                  pl.BlockSpec((1, d), lambda i, j: (0, 0)),
                  pl.BlockSpec((d, n), lambda i, j: (0, 0))],
        out_specs=pl.BlockSpec((1, TM, n), lambda i, j: (i, j, 0)),
        out_shape=jax.ShapeDtypeStruct((b, t, n), F32),
        compiler_params=_cp(("parallel", "arbitrary")),
        name="in_projection",
    )(x, mod, gain.reshape(1, d), w)


def _na_block_start(j):
    return jnp.clip(NA_RB * (j - 1) - NA_WIN_R // 2, 0, GRID_W - NA_KROWS)


def na_bias_tables(rpb):
    rows = GRID_W
    n_dr = 2 * NA_WIN_R - 1
    cq = np.arange(GRID_W)[:, None]
    ck = np.arange(GRID_W)[None, :]
    c0 = np.clip(cq - NA_WIN_C // 2, 0, GRID_W - NA_WIN_C)
    ok_c = (ck >= c0) & (ck < c0 + NA_WIN_C)
    ic = np.clip(ck - cq + NA_WIN_C - 1, 0, 2 * NA_WIN_C - 2)
    tiles = jnp.where(ok_c[None, None], rpb.astype(F32)[:, :, ic], NEG)
    tiles = jnp.concatenate([tiles, jnp.full_like(tiles[:, :1], NEG)], axis=1)
    idx = np.full((4, NA_RB, NA_KROWS), n_dr, np.int32)
    for var, rb in enumerate((0, 1, rows // NA_RB - 1)):
        start = int(np.clip(NA_RB * rb - NA_WIN_R // 2, 0, rows - NA_KROWS))
        for qi in range(NA_RB):
            rq = NA_RB * rb + qi
            r0 = int(np.clip(rq - NA_WIN_R // 2, 0, rows - NA_WIN_R))
            for kj in range(NA_KROWS):
                rk = start + kj
                if r0 <= rk < r0 + NA_WIN_R:
                    idx[var, qi, kj] = rk - rq + NA_WIN_R - 1
    full = tiles[:, idx]
    return full.transpose(0, 1, 2, 4, 3, 5).reshape(rpb.shape[0], 4, NA_QB, NA_KU)


def _na_kernel(q_ref, k_ref, v_ref, qg_ref, kg_ref, bias_ref, o_ref, kn_ref):
    j = pl.program_id(2)
    is_h0 = lax.broadcasted_iota(jnp.int32, (1, LANES), 1) < NA_DH

    def headnorm(x, gain):
        x2 = x * x
        s0 = jnp.sum(jnp.where(is_h0, x2, 0.0), axis=-1, keepdims=True)
        s1 = jnp.sum(jnp.where(is_h0, 0.0, x2), axis=-1, keepdims=True)
        ms = jnp.where(is_h0, s0, s1) * (1.0 / NA_DH)
        return (x * lax.rsqrt(ms + EPS)) * gain

    @pl.when(j == 0)
    def _():
        kn_ref[...] = headnorm(k_ref[0], kg_ref[...]).astype(MXU_DT)

    qn = headnorm(q_ref[0], qg_ref[...]) * (NA_DH ** -0.5)
    off = pl.multiple_of(CTX_LEN + _na_block_start(j) * GRID_W, GRID_W)
    k_loc = kn_ref[pl.ds(off, NA_KU), :]
    k_ctx = kn_ref[0:CTX_LEN, :]
    v_loc = v_ref[0, pl.ds(off, NA_KU), :].astype(MXU_DT)
    v_ctx = v_ref[0, 0:CTX_LEN, :].astype(MXU_DT)
    outs = []
    for h in range(2):
        qm = jnp.where(is_h0 if h == 0 else jnp.logical_not(is_h0), qn, 0.0)
        s_loc = _mm_nt(qm, k_loc) + bias_ref[h, 0]
        s_ctx = _mm_nt(qm, k_ctx)
        m = jnp.maximum(jnp.max(s_loc, axis=-1, keepdims=True), jnp.max(s_ctx, axis=-1, keepdims=True))
        p_loc = jnp.exp(s_loc - m)
        p_ctx = jnp.exp(s_ctx - m)
        den = jnp.sum(p_loc, axis=-1, keepdims=True) + jnp.sum(p_ctx, axis=-1, keepdims=True)
        outs.append((_mm(p_loc, v_loc) + _mm(p_ctx, v_ctx)) / den)
    o_ref[0] = jnp.where(is_h0, outs[0], outs[1])


def neighbourhood_attention(proj, q_gain, k_gain, bias):
    b, t, _ = proj.shape
    nq = t // NA_QB
    n_rb = GRID_W // NA_RB

    def bias_idx(i, hp, j):
        var = jnp.where(j == 0, 3, jnp.where(j == 1, 0, jnp.where(j == n_rb, 2, 1)))
        return (hp, var, 0, 0)

    tile2 = lambda g: jnp.tile(g.astype(F32), 2).reshape(1, LANES)
    return pl.pallas_call(
        _na_kernel,
        grid=(b, NA_HEADS // 2, nq),
        in_specs=[pl.BlockSpec((1, NA_QB, LANES), lambda i, hp, j: (i, j, E_Q // LANES + hp)),
                  pl.BlockSpec((1, t, LANES), lambda i, hp, j: (i, 0, E_K // LANES + hp)),
                  pl.BlockSpec((1, t, LANES), lambda i, hp, j: (i, 0, E_V // LANES + hp)),
                  pl.BlockSpec((1, LANES), lambda i, hp, j: (0, 0)),
                  pl.BlockSpec((1, LANES), lambda i, hp, j: (0, 0)),
                  pl.BlockSpec((2, 1, NA_QB, NA_KU), bias_idx)],
        out_specs=pl.BlockSpec((1, NA_QB, LANES), lambda i, hp, j: (i, j, hp)),
        out_shape=jax.ShapeDtypeStruct((b, t, NA_W), F32),
        scratch_shapes=[pltpu.VMEM((t, LANES), MXU_DT)],
        compiler_params=_cp(("parallel", "arbitrary", "arbitrary")),
        name="neighbourhood_attention",
    )(proj, proj, proj, tile2(q_gain), tile2(k_gain), bias)


def _scan_tile(i, n_tiles, reverse):
    if not reverse:
        return i
    return jnp.where(i == 0, 0, n_tiles - i)


def _tri_masks(n, reverse):
    r = lax.broadcasted_iota(jnp.int32, (n, n), 0)
    c = lax.broadcasted_iota(jnp.int32, (n, n), 1)
    if reverse:
        return c >= r, c > r, r >= c
    return c <= r, c < r, r <= c


def _cumsum_both(x_col, x_row, incl, incl_t):
    cum_col = jnp.sum(jnp.where(incl, x_row, 0.0), axis=1, keepdims=True)
    cum_row = jnp.sum(jnp.where(incl_t, x_col, 0.0), axis=0, keepdims=True)
    return cum_col, cum_row


def _rope(x, cos, sin):
    w = x.shape[-1]
    lane = lax.broadcasted_iota(jnp.int32, (1, w), 1)
    first = (lane % ML_DQK) < ML_DQK // 2
    swapped = jnp.where(first, pltpu.roll(x, w - ML_DQK // 2, 1), pltpu.roll(x, ML_DQK // 2, 1))
    return x * cos + swapped * sin


def _mlstm_kernel(q_ref, k_ref, v_ref, g_ref, gt_ref, gbr_ref, gbc_ref, cos_ref, sin_ref, o_ref,
                  c_ref, n_ref, m_ref, *, reverse):
    i = pl.program_id(1)
    d = 1 if reverse else 0

    @pl.when(i == 0)
    def _():
        c_ref[...] = jnp.zeros_like(c_ref)
        n_ref[...] = jnp.zeros_like(n_ref)
        m_ref[...] = jnp.zeros_like(m_ref)

    q = _rope(q_ref[0], cos_ref[...], sin_ref[...])
    k = _rope(k_ref[0] * (ML_DQK ** -0.5), cos_ref[...], sin_ref[...])
    g_col = g_ref[0] + gbr_ref[...]
    g_row = gt_ref[0] + gbc_ref[...]
    incl, _, incl_t = _tri_masks(ML_CHUNK, reverse)
    is_h0 = lax.broadcasted_iota(jnp.int32, (1, LANES), 1) < ML_DQK
    nc = TM // ML_CHUNK
    nh = ML_HEADS
    order = range(nc - 1, -1, -1) if reverse else range(nc)
    tokc = lambda ci: slice(ci * ML_CHUNK, (ci + 1) * ML_CHUNK)
    pair = lambda h: slice((h // 2) * LANES, (h // 2 + 1) * LANES)
    hmask = lambda h: is_h0 if h % 2 == 0 else jnp.logical_not(is_h0)
    stack = lambda fn: jnp.stack([fn(h, ci) for h in range(nh) for ci in range(nc)], axis=0)
    by_head = lambda a: a.reshape((nh, nc) + a.shape[1:])
    flat = lambda parts: jnp.stack(parts, axis=1).reshape((nh * nc,) + parts[0].shape[1:])
    qm = stack(lambda h, ci: jnp.where(hmask(h), q[tokc(ci), pair(h)], 0.0))
    km = stack(lambda h, ci: jnp.where(hmask(h), k[tokc(ci), pair(h)], 0.0))
    v = stack(lambda h, ci: v_ref[0, tokc(ci), h * ML_DV:(h + 1) * ML_DV])
    li_c = stack(lambda h, ci: g_col[tokc(ci), 8 * d + h:8 * d + h + 1])
    lf_c = jax.nn.log_sigmoid(stack(lambda h, ci: g_col[tokc(ci), 8 * d + 4 + h:8 * d + 5 + h]))
    li_r = stack(lambda h, ci: g_row[8 * d + h:8 * d + h + 1, tokc(ci)])
    lf_r = jax.nn.log_sigmoid(stack(lambda h, ci: g_row[8 * d + 4 + h:8 * d + 5 + h, tokc(ci)]))
    cum_c = jnp.sum(jnp.where(incl, lf_r, 0.0), axis=-1, keepdims=True)
    cum_r = jnp.sum(jnp.where(incl_t, lf_c, 0.0), axis=-2, keepdims=True)
    log_d = jnp.where(incl, cum_c - cum_r + li_r, NEG)
    m_loc = jnp.max(log_d, axis=-1, keepdims=True)
    cum_last = jnp.sum(lf_r, axis=-1, keepdims=True)
    log_w = cum_last - cum_c + li_c
    m_w = jnp.max(log_w, axis=-2, keepdims=True)
    cl4, mw4 = by_head(cum_last), by_head(m_w)
    m = m_ref[...][:, :, 0:1]
    m_prev, m_new = [None] * nc, [None] * nc
    for ci in order:
        m_prev[ci] = m
        m = jnp.maximum(cl4[:, ci] + m, mw4[:, ci])
        m_new[ci] = m
    m_ref[...] = jnp.broadcast_to(m, m_ref.shape)
    m_prev, m_new = flat(m_prev), flat(m_new)
    m_inter = cum_c + m_prev
    m_t = jnp.maximum(m_loc, m_inter)
    sg = _bmm_nt(qm, km) * jnp.exp(log_d - m_t)
    w_inter = jnp.exp(m_inter - m_t)
    kw = km * jnp.exp(log_w - m_new)
    upd_c, upd_n = by_head(_bmm_tn(kw, v)), by_head(jnp.sum(kw, axis=-2, keepdims=True))
    decay = by_head(jnp.exp(cum_last + m_prev - m_new))
    c_mat, n_vec = c_ref[...], n_ref[...]
    c_prev, n_prev = [None] * nc, [None] * nc
    for ci in order:
        c_prev[ci], n_prev[ci] = c_mat, n_vec
        c_mat = decay[:, ci] * c_mat + upd_c[:, ci]
        n_vec = decay[:, ci] * n_vec + upd_n[:, ci]
    c_ref[...] = c_mat
    n_ref[...] = n_vec
    num = _bmm(sg, v) + w_inter * _bmm(qm, flat(c_prev))
    den = jnp.sum(sg, axis=-1, keepdims=True) + w_inter * jnp.sum(qm * flat(n_prev), axis=-1, keepdims=True)
    h_out = by_head(num / jnp.maximum(jnp.abs(den), jnp.exp(-m_t)))
    for h in range(nh):
        for ci in range(nc):
            o_ref[0, 0, tokc(ci), h * ML_DV:(h + 1) * ML_DV] = h_out[h, ci]


def mlstm_direction(proj, gates_t, gate_b, cos, sin, reverse):
    b, t, _ = proj.shape
    n_tiles = t // TM
    tile = functools.partial(_scan_tile, n_tiles=n_tiles, reverse=reverse)
    gb = gate_b.astype(F32).reshape(16)
    gb_row = jnp.zeros((1, LANES), F32).at[0, :16].set(gb)
    gb_col = gb.reshape(16, 1)
    return pl.pallas_call(
        functools.partial(_mlstm_kernel, reverse=reverse),
        grid=(b, n_tiles),
        in_specs=[pl.BlockSpec((1, TM, ML_QK_W), lambda bi, i: (bi, tile(i), E_MQ // ML_QK_W)),
                  pl.BlockSpec((1, TM, ML_QK_W), lambda bi, i: (bi, tile(i), E_MK // ML_QK_W)),
                  pl.BlockSpec((1, TM, ML_V_W), lambda bi, i: (bi, tile(i), E_MV // ML_V_W)),
                  pl.BlockSpec((1, TM, LANES), lambda bi, i: (bi, tile(i), E_MG // LANES)),
                  pl.BlockSpec((1, 16, TM), lambda bi, i: (bi, 0, tile(i))),
                  pl.BlockSpec((1, LANES), lambda bi, i: (0, 0)),
                  pl.BlockSpec((16, 1), lambda bi, i: (0, 0)),
                  pl.BlockSpec((TM, ML_QK_W), lambda bi, i: (tile(i), 0)),
                  pl.BlockSpec((TM, ML_QK_W), lambda bi, i: (tile(i), 0))],
        out_specs=pl.BlockSpec((1, 1, TM, ML_V_W), lambda bi, i: (0, bi, tile(i), 0)),
        out_shape=jax.ShapeDtypeStruct((1, b, t, ML_V_W), F32),
        scratch_shapes=[pltpu.VMEM((ML_HEADS, LANES, ML_DV), F32),
                        pltpu.VMEM((ML_HEADS, 1, LANES), F32),
                        pltpu.VMEM((ML_HEADS, 1, LANES), F32)],
        compiler_params=_cp(("parallel", "arbitrary")),
        name="mlstm_bwd" if reverse else "mlstm_fwd",
    )(proj, proj, proj, proj, gates_t, gb_row, gb_col, cos, sin)[0]


def rope_tables(t):
    s = t - CTX_LEN
    tok = jnp.arange(s)
    n_freq = ML_DQK // 4
    inv = ROPE_THETA ** (-jnp.arange(n_freq, dtype=F32) / n_freq)
    ang = jnp.concatenate([(tok // GRID_W).astype(F32)[:, None] * inv,
                           (tok % GRID_W).astype(F32)[:, None] * inv], axis=-1)
    cos, sin = jnp.cos(ang), jnp.sin(ang)
    cos_h = jnp.concatenate([cos, cos], axis=-1)
    sin_h = jnp.concatenate([-sin, sin], axis=-1)
    cos_f = jnp.concatenate([jnp.ones((CTX_LEN, ML_DQK), F32), cos_h], axis=0)
    sin_f = jnp.concatenate([jnp.zeros((CTX_LEN, ML_DQK), F32), sin_h], axis=0)
    return jnp.tile(cos_f, (1, ML_HEADS)), jnp.tile(sin_f, (1, ML_HEADS))


def _dnconv_kernel(x_ref, w_ref, o_ref):
    cb = pl.program_id(1)
    x = x_ref[0]
    t = x.shape[0]
    tok = lax.broadcasted_iota(jnp.int32, (t, 1), 0)
    seg_lo = jnp.where(tok < CTX_LEN, 0, CTX_LEN)
    seg_hi = jnp.where(tok < CTX_LEN, CTX_LEN, t)
    acc = jnp.zeros_like(x)
    for j in range(DN_CONV):
        o = j - DN_CONV // 2
        xs = x if o == 0 else pltpu.roll(x, (-o) % t, 0)
        ok = (tok + o >= seg_lo) & (tok + o < seg_hi)
        acc = acc + jnp.where(ok, xs, 0.0) * w_ref[j:j + 1, :]
    y = _silu(acc)
    nrm = lax.rsqrt(jnp.sum(y * y, axis=-1, keepdims=True) + EPS)
    q_heads = DN_QK_W // LANES
    o_ref[0] = jnp.where(cb < q_heads, (y * nrm) * (DN_DK ** -0.5), jnp.where(cb < 2 * q_heads, y * nrm, y))


def dn_conv(proj, conv_w):
    b, t, _ = proj.shape
    return pl.pallas_call(
        _dnconv_kernel,
        grid=(b, DN_CONV_CH // LANES),
        in_specs=[pl.BlockSpec((1, t, LANES), lambda i, c: (i, 0, c)),
                  pl.BlockSpec((DN_CONV, LANES), lambda i, c: (0, c))],
        out_specs=pl.BlockSpec((1, t, LANES), lambda i, c: (i, 0, c)),
        out_shape=jax.ShapeDtypeStruct((b, t, DN_CONV_CH), F32),
        compiler_params=_cp(("parallel", "arbitrary")),
        name="dn_conv",
    )(proj, conv_w.astype(F32).T)


def _gdn_kernel(q_ref, k_ref, v_ref, ab_ref, at_ref, par_ref, parc_ref, o_ref, s_ref, *, reverse):
    i = pl.program_id(1)
    d = 1 if reverse else 0

    @pl.when(i == 0)
    def _():
        s_ref[...] = jnp.zeros_like(s_ref)

    incl, strict, incl_t = _tri_masks(DN_CHUNK, reverse)
    r = lax.broadcasted_iota(jnp.int32, (DN_CHUNK, DN_CHUNK), 0)
    c = lax.broadcasted_iota(jnp.int32, (DN_CHUNK, DN_CHUNK), 1)
    eye = (r == c).astype(F32)
    ab = ab_ref[0]
    g_all = -jnp.exp(par_ref[0:1, :]) * jax.nn.softplus(ab + par_ref[1:2, :])
    beta_full = jax.nn.sigmoid(ab)
    gt_all = -jnp.exp(parc_ref[:, 0:1]) * jax.nn.softplus(at_ref[0] + parc_ref[:, 1:2])

    nc = TM // DN_CHUNK
    nh = DN_HEADS
    tokc = lambda ci: slice(ci * DN_CHUNK, (ci + 1) * DN_CHUNK)
    hl = lambda h: slice(h * LANES, (h + 1) * LANES)
    stack = lambda fn: jnp.stack([fn(h, ci) for h in range(nh) for ci in range(nc)], axis=0)
    q = stack(lambda h, ci: q_ref[0, tokc(ci), hl(h)])
    k = stack(lambda h, ci: k_ref[0, tokc(ci), hl(h)])
    v = stack(lambda h, ci: v_ref[0, tokc(ci), hl(h)])
    g_c = stack(lambda h, ci: g_all[tokc(ci), 8 * d + h:8 * d + h + 1])
    beta = stack(lambda h, ci: beta_full[tokc(ci), 16 + 8 * d + h:17 + 8 * d + h])
    g_r = stack(lambda h, ci: gt_all[8 * d + h:8 * d + h + 1, tokc(ci)])
    gc_c = jnp.sum(jnp.where(incl, g_r, 0.0), axis=-1, keepdims=True)
    gc_r = jnp.sum(jnp.where(incl_t, g_c, 0.0), axis=-2, keepdims=True)
    decay = jnp.exp(jnp.where(incl, gc_c - gc_r, NEG))
    kb = k * beta
    a_mat = jnp.where(strict, _bmm_nt(kb, k) * decay, 0.0)
    pw = -a_mat
    t_inv = eye + pw
    for _ in range(5):
        pw = _bmm(pw, pw)
        t_inv = t_inv + _bmm(t_inv, pw)
    egc = jnp.exp(gc_c)
    uw = _bmm(t_inv, jnp.concatenate([v * beta, kb * egc], axis=-1))
    attn = _bmm_nt(q, k) * decay
    g_last = jnp.sum(g_c, axis=-2, keepdims=True)
    kg = k * jnp.exp(g_last - gc_c)
    ng = _bmm_tn(kg, uw)
    au = _bmm(attn, uw)
    lhs = jnp.concatenate([ng[:, :, DN_DV:], q * egc - au[:, :, DN_DV:]], axis=1)
    by_head = lambda a: a.reshape((nh, nc) + a.shape[1:])
    lhs, n_mat, o2, dec = by_head(lhs), by_head(ng[:, :, :DN_DV]), by_head(au[:, :, :DN_DV]), by_head(jnp.exp(g_last))
    s_mat = s_ref[...]
    for ci in (range(nc - 1, -1, -1) if reverse else range(nc)):
        prod = _bmm(lhs[:, ci], s_mat)
        o = prod[:, DN_DK:] + o2[:, ci]
        for h in range(nh):
            o_ref[0, 0, tokc(ci), hl(h)] = o[h]
        s_mat = s_mat * dec[:, ci] + n_mat[:, ci] - prod[:, :DN_DK]
    s_ref[...] = s_mat


def gdn_direction(qkv, proj, gates_t, a_log, dt_bias, reverse):
    b, t, _ = qkv.shape
    n_tiles = t // TM
    tile = functools.partial(_scan_tile, n_tiles=n_tiles, reverse=reverse)
    par = jnp.zeros((2, LANES), F32)
    par = par.at[0, :16].set(a_log.astype(F32).reshape(16)).at[1, :16].set(dt_bias.astype(F32).reshape(16))
    parc = par[:, :16].T
    return pl.pallas_call(
        functools.partial(_gdn_kernel, reverse=reverse),
        grid=(b, n_tiles),
        in_specs=[pl.BlockSpec((1, TM, DN_QK_W), lambda bi, i: (bi, tile(i), 0)),
                  pl.BlockSpec((1, TM, DN_QK_W), lambda bi, i: (bi, tile(i), 1)),
                  pl.BlockSpec((1, TM, DN_V_W), lambda bi, i: (bi, tile(i), 2)),
                  pl.BlockSpec((1, TM, LANES), lambda bi, i: (bi, tile(i), O_A // LANES)),
                  pl.BlockSpec((1, 16, TM), lambda bi, i: (bi, 0, tile(i))),
                  pl.BlockSpec((2, LANES), lambda bi, i: (0, 0)),
                  pl.BlockSpec((16, 2), lambda bi, i: (0, 0))],
        out_specs=pl.BlockSpec((1, 1, TM, DN_V_W), lambda bi, i: (0, bi, tile(i), 0)),
        out_shape=jax.ShapeDtypeStruct((1, b, t, DN_V_W), F32),
        scratch_shapes=[pltpu.VMEM((DN_HEADS, DN_DK, DN_DV), F32)],
        compiler_params=_cp(("parallel", "arbitrary")),
        name="gdn_bwd" if reverse else "gdn_fwd",
    )(qkv, qkv, qkv, proj, gates_t, par, parc)[0]


def _headwise_rms(x, gain, width):
    parts = []
    for h0 in range(0, x.shape[-1], width):
        xh = x[:, h0:h0 + width]
        parts.append((xh * lax.rsqrt(jnp.mean(xh * xh, axis=-1, keepdims=True) + EPS)) * gain)
    return parts


def _outproj_kernel(*refs, even):
    if even:
        (a_ref, hf_ref, hb_ref, og_ref, gain_ref, x_ref, mod_ref, g2_ref, w_ref, rw_ref, rb_ref,
         xo_ref, hm_ref, tv_ref, ti_ref) = refs
        hs = _headwise_rms(hf_ref[0] + hb_ref[0], gain_ref[...], ML_DV)
        og = og_ref[0]
        mix = [a_ref[0]] + [hs[h] * jax.nn.sigmoid(og[:, h * ML_DV:(h + 1) * ML_DV]) for h in range(ML_HEADS)]
    else:
        (of_ref, ob_ref, z_ref, gain_ref, x_ref, mod_ref, g2_ref, w_ref, rw_ref, rb_ref,
         xo_ref, hm_ref, tv_ref, ti_ref) = refs
        hs = _headwise_rms(of_ref[0] + ob_ref[0], gain_ref[...], DN_DV)
        z = z_ref[0]
        mix = [hs[h] * _silu(z[:, h * DN_DV:(h + 1) * DN_DV]) for h in range(DN_HEADS)]
    mix = jnp.concatenate(mix, axis=-1)
    x_new = x_ref[0] + mod_ref[2:3, :] * _mm(mix, w_ref[...])
    xo_ref[0] = x_new
    hm = _norm_mod(x_new, g2_ref[...], mod_ref[3:4, :], mod_ref[4:5, :])
    hm_ref[0] = hm.astype(hm_ref.dtype)
    rw = rw_ref[...]
    hm_hi, rw_hi = hm.astype(MXU_DT), rw.astype(MXU_DT)
    hm_lo, rw_lo = hm - hm_hi.astype(F32), rw - rw_hi.astype(F32)
    logits = (_mm(hm_hi, rw_hi) + (_mm(hm_hi, rw_lo) + _mm(hm_lo, rw_hi))) + rb_ref[...]
    lane = lax.broadcasted_iota(jnp.int32, logits.shape, 1)
    vals = jnp.zeros_like(logits)
    idxs = jnp.zeros(logits.shape, jnp.int32)
    work = jnp.where(lane < N_EXPERTS, logits, -jnp.inf)
    for kk in range(TOP_K):
        mx = jnp.max(work, axis=-1, keepdims=True)
        am = jnp.min(jnp.where(work == mx, lane, LANES), axis=-1, keepdims=True)
        vals = jnp.where(lane == kk, mx, vals)
        idxs = jnp.where(lane == kk, am, idxs)
        work = jnp.where(lane == am, -jnp.inf, work)
    ex = jnp.where(lane < TOP_K, jnp.exp(vals - jnp.max(jnp.where(lane < TOP_K, vals, -jnp.inf), axis=-1, keepdims=True)), 0.0)
    tv_ref[0] = ex / jnp.sum(ex, axis=-1, keepdims=True)
    ti_ref[0] = idxs


def out_projection(even, mixer_in, gain, x, mod, g2, w_out, router_w, router_b):
    b, t, d = x.shape
    tok = lambda wdt, blk: pl.BlockSpec((1, TM, wdt), lambda i, j: (i, j, blk))
    if even:
        a, hf, hb, proj = mixer_in
        ins = [a, hf, hb, proj]
        specs = [tok(NA_W, 0), tok(ML_V_W, 0), tok(ML_V_W, 0), tok(ML_V_W, E_MO // ML_V_W)]
    else:
        of, ob, proj = mixer_in
        ins = [of, ob, proj]
        specs = [tok(DN_V_W, 0), tok(DN_V_W, 0), tok(DN_V_W, O_Z // DN_V_W)]
    rw = jnp.zeros((d, LANES), F32).at[:, :N_EXPERTS].set(router_w.astype(F32))
    rb = jnp.zeros((1, LANES), F32).at[0, :N_EXPERTS].set(router_b.astype(F32))
    const = lambda shape: pl.BlockSpec(shape, lambda i, j: (0,) * len(shape))
    ins += [gain.astype(F32).reshape(1, LANES), x, mod, g2.reshape(1, d), w_out.astype(MXU_DT), rw, rb]
    specs += [const((1, LANES)), tok(d, 0),
              pl.BlockSpec((None, None, 6, d), lambda i, j: (i, jnp.minimum(j, 1), 0, 0)),
              const((1, d)), const(w_out.shape), const((d, LANES)), const((1, LANES))]
    return pl.pallas_call(
        functools.partial(_outproj_kernel, even=even),
        grid=(b, t // TM),
        in_specs=specs,
        out_specs=[tok(d, 0), tok(d, 0), tok(LANES, 0), tok(LANES, 0)],
        out_shape=[jax.ShapeDtypeStruct((b, t, d), F32), jax.ShapeDtypeStruct((b, t, d), F32),
                   jax.ShapeDtypeStruct((b, t, LANES), F32), jax.ShapeDtypeStruct((b, t, LANES), jnp.int32)],
        compiler_params=_cp(("parallel", "arbitrary")),
        name="out_projection_even" if even else "out_projection_odd",
    )(*ins)


MOE_GRP = 2 * LANES


def _swiglu_perm():
    p = np.zeros((MOE_GRP, MOE_GRP), np.float32)
    j = np.arange(LANES)
    p[2 * j, j] = 1.0
    p[2 * j + 1, LANES + j] = 1.0
    return p


def _swiglu_col_order(n):
    g = np.arange(n // MOE_GRP)[:, None] * MOE_GRP
    j = np.arange(LANES)[None, :]
    return np.concatenate([g + 2 * j, g + 2 * j + 1], axis=1).reshape(-1)


def _experts_kernel(be_ref, nb_ref, x_ref, w1_ref, b1_ref, w2_ref, b2_ref, perm_ref, o_ref, w1s_ref, w2s_ref):
    i = pl.program_id(0)
    n_grp = w1s_ref.shape[1] // MOE_GRP

    @pl.when((i == 0) | (be_ref[i] != be_ref[jnp.maximum(i - 1, 0)]))
    def _():
        for g in range(n_grp):
            cols = slice(g * MOE_GRP, (g + 1) * MOE_GRP)
            w1s_ref[:, cols] = jnp.dot(w1_ref[0, :, cols].astype(MXU_DT), perm_ref[...],
                                       preferred_element_type=F32).astype(MXU_DT)
        w2s_ref[...] = w2_ref[0].astype(MXU_DT)

    @pl.when(i < nb_ref[0])
    def _():
        x = x_ref[...].astype(MXU_DT)
        acts = []
        for g in range(n_grp):
            cols = slice(g * MOE_GRP, (g + 1) * MOE_GRP)
            hid = jnp.dot(x, w1s_ref[:, cols], preferred_element_type=F32) + b1_ref[0, :, cols]
            glu = jnp.minimum(hid[:, :LANES], SWIGLU_LIMIT)
            lin = jnp.clip(hid[:, LANES:], -SWIGLU_LIMIT, SWIGLU_LIMIT)
            acts.append((glu * jax.nn.sigmoid(SWIGLU_ALPHA * glu) * (lin + 1.0)).astype(MXU_DT))
        act = jnp.concatenate(acts, axis=-1)
        o_ref[...] = (jnp.dot(act, w2s_ref[...], preferred_element_type=F32) + b2_ref[0]).astype(o_ref.dtype)

    @pl.when(i >= nb_ref[0])
    def _():
        o_ref[...] = jnp.zeros_like(o_ref)


def experts_ffn(xs, block_exp, n_used, layer, w1, b1p, w2, b2):
    n_slots, d = xs.shape
    n_blocks = n_slots // MOE_BM
    dh2 = w1.shape[-1]
    wspec = lambda shape: pl.BlockSpec((None, 1) + shape, lambda i, be, nb: (layer, be[i], 0, 0))
    bspec = lambda shape: pl.BlockSpec((1,) + shape, lambda i, be, nb: (be[i], 0, 0))
    return pl.pallas_call(
        _experts_kernel,
        grid_spec=pltpu.PrefetchScalarGridSpec(
            num_scalar_prefetch=2,
            grid=(n_blocks,),
            in_specs=[pl.BlockSpec((MOE_BM, d), lambda i, be, nb: (jnp.minimum(i, nb[0] - 1), 0)),
                      wspec((d, dh2)), bspec((1, dh2)), wspec((dh2 // 2, d)), bspec((1, d)),
                      pl.BlockSpec((MOE_GRP, MOE_GRP), lambda i, be, nb: (0, 0))],
            out_specs=pl.BlockSpec((MOE_BM, d), lambda i, be, nb: (i, 0)),
            scratch_shapes=[pltpu.VMEM((d, dh2), MXU_DT), pltpu.VMEM((dh2 // 2, d), MXU_DT)],
        ),
        out_shape=jax.ShapeDtypeStruct((n_slots, d), MXU_DT),
        compiler_params=_cp(("arbitrary",)),
        name="experts_ffn",
    )(block_exp, n_used, xs, w1, b1p, w2, b2, jnp.asarray(_swiglu_perm(), MXU_DT))


def _combine_kernel(x_ref, y_ref, gate_ref, mod_ref, o_ref):
    gate = gate_ref[0]
    f = y_ref[0, 0].astype(F32) * gate[:, 0:1]
    for kk in range(1, TOP_K):
        f = f + y_ref[0, kk].astype(F32) * gate[:, kk:kk + 1]
    o_ref[0] = x_ref[0] + mod_ref[5:6, :] * f


def moe_combine(x, yg, gate, mod):
    b, t, d = x.shape
    return pl.pallas_call(
        _combine_kernel,
        grid=(b, t // TM),
        in_specs=[pl.BlockSpec((1, TM, d), lambda i, j: (i, j, 0)),
                  pl.BlockSpec((1, TOP_K, TM, d), lambda i, j: (i, 0, j, 0)),
                  pl.BlockSpec((1, TM, LANES), lambda i, j: (i, j, 0)),
                  pl.BlockSpec((None, None, 6, d), lambda i, j: (i, jnp.minimum(j, 1), 0, 0))],
        out_specs=pl.BlockSpec((1, TM, d), lambda i, j: (i, j, 0)),
        out_shape=jax.ShapeDtypeStruct((b, t, d), F32),
        compiler_params=_cp(("parallel", "arbitrary")),
        name="moe_combine",
    )(x, yg, gate, mod)


def moe_layer(x_new, hm, gate, top_idx, mod, layer, w1, b1, w2, b2):
    b, t, d = x_new.shape
    n_tok = b * t
    n_asg = n_tok * TOP_K
    flat_exp = top_idx[:, :, :TOP_K].transpose(0, 2, 1).reshape(n_asg)
    cb = 256
    onehot = (flat_exp[:, None] == jnp.arange(N_EXPERTS, dtype=jnp.int32)[None, :]).astype(F32)
    onehot = onehot.reshape(n_asg // cb, cb, N_EXPERTS)
    within = jnp.einsum('ij,bjk->bik', jnp.tril(jnp.ones((cb, cb), F32)), onehot)
    blk_tot = within[:, -1, :]
    blk_off = jnp.cumsum(blk_tot, axis=0) - blk_tot
    counts = (blk_off[-1] + blk_tot[-1]).astype(jnp.int32)
    rank = jnp.sum((within - onehot + blk_off[:, None, :]) * onehot, axis=-1).reshape(n_asg).astype(jnp.int32)
    padded = (counts + MOE_BM - 1) // MOE_BM * MOE_BM
    pad_end = jnp.cumsum(padded)
    dest = (pad_end - padded)[flat_exp] + rank
    n_blocks = -(-n_asg // MOE_BM) + N_EXPERTS
    n_slots = n_blocks * MOE_BM
    block_start = jnp.arange(n_blocks, dtype=jnp.int32) * MOE_BM
    block_exp = jnp.minimum(jnp.sum((pad_end[None, :] <= block_start[:, None]).astype(jnp.int32), axis=1),
                            N_EXPERTS - 1)
    n_used = (pad_end[-1] // MOE_BM).astype(jnp.int32).reshape(1)
    asg_bits = (n_asg - 1).bit_length()
    assert N_EXPERTS << asg_bits < 2 ** 31
    sorted_asg = jnp.sort((flat_exp << asg_bits) + jnp.arange(n_asg, dtype=jnp.int32)) & ((1 << asg_bits) - 1)
    first_asg = jnp.cumsum(counts) - counts
    in_exp = (block_start - (pad_end - padded)[block_exp])[:, None] + jnp.arange(MOE_BM, dtype=jnp.int32)[None, :]
    is_row = in_exp < counts[block_exp][:, None]
    asg = sorted_asg[jnp.clip(first_asg[block_exp][:, None] + in_exp, 0, n_asg - 1).reshape(n_slots)]
    tok = (asg // (TOP_K * t)) * t + asg % t
    slot_tok = jnp.where(is_row.reshape(n_slots), tok, jnp.arange(n_slots, dtype=jnp.int32) % n_tok)
    xs = hm.reshape(n_tok, d)[slot_tok]
    b1p = b1.astype(F32)[:, _swiglu_col_order(b1.shape[-1])][:, None, :]
    y = experts_ffn(xs, block_exp, n_used, layer, w1, b1p, w2, b2[:, None, :].astype(F32))
    yg = y[dest].reshape(b, TOP_K, t, d)
    return moe_combine(x_new, yg, gate, mod)


def _pad_cols(w, n):
    return jnp.pad(w, ((0, 0), (0, n - w.shape[1]))).astype(MXU_DT)


N_STREAMS = 2


def kernel(x, c, ctx, c_ctx, ada_w, ada_b, norm1_g, norm2_g, ev_w_in, ev_w_out, na_q_gain, na_k_gain, na_rpb,
           ml_gate_b, ml_out_gain, od_w_in, od_w_out, dn_conv_w, dn_a_log, dn_dt_bias, dn_out_gain, router_w,
           router_b, exp_w1, exp_b1, exp_w2, exp_b2):
    b = x.shape[0]
    assert b % N_STREAMS == 0
    g = b // N_STREAMS
    outs = [_trunk(x[i:i + g], c[i:i + g], ctx[i:i + g], c_ctx, ada_w, ada_b, norm1_g, norm2_g, ev_w_in, ev_w_out,
                   na_q_gain, na_k_gain, na_rpb, ml_gate_b, ml_out_gain, od_w_in, od_w_out, dn_conv_w, dn_a_log,
                   dn_dt_bias, dn_out_gain, router_w, router_b, exp_w1, exp_b1, exp_w2, exp_b2)
            for i in range(0, b, g)]
    return jnp.concatenate(outs, axis=0)


def _trunk(x, c, ctx, c_ctx, ada_w, ada_b, norm1_g, norm2_g, ev_w_in, ev_w_out, na_q_gain, na_k_gain, na_rpb,
           ml_gate_b, ml_out_gain, od_w_in, od_w_out, dn_conv_w, dn_a_log, dn_dt_bias, dn_out_gain, router_w,
           router_b, exp_w1, exp_b1, exp_w2, exp_b2):
    b, s, d = x.shape
    depth = ada_w.shape[0]
    t = CTX_LEN + s
    xs = jnp.concatenate([ctx, x], axis=1)
    cond = jnp.zeros((16, d), F32).at[:b].set(c).at[b].set(c_ctx)
    mods = adaln_all(cond, ada_w, ada_b)
    mod_lat = mods[:, :b].reshape(depth, b, 1, 6, d)
    mod_ctx = jnp.broadcast_to(mods[:, b].reshape(depth, 1, 1, 6, d), (depth, b, 1, 6, d))
    mod_all = jnp.concatenate([mod_ctx, mod_lat], axis=2)
    cos, sin = rope_tables(t)
    for layer in range(depth):
        mod = mod_all[layer]
        if layer % 2 == 0:
            e = layer // 2
            proj = in_projection(xs, mod, norm1_g[layer], _pad_cols(ev_w_in[e], EVEN_PAD), INPROJ_CHUNK)
            a = neighbourhood_attention(proj, na_q_gain[e], na_k_gain[e], na_bias_tables(na_rpb[e]))
            gates_t = proj[:, :, E_MG:E_MG + 16].transpose(0, 2, 1)
            hf = mlstm_direction(proj, gates_t, ml_gate_b[e], cos, sin, False)
            hb = mlstm_direction(proj, gates_t, ml_gate_b[e], cos, sin, True)
            mixer_in, gain, w_out = (a, hf, hb, proj), ml_out_gain[e], ev_w_out[e]
        else:
            o = layer // 2
            proj = in_projection(xs, mod, norm1_g[layer], _pad_cols(od_w_in[o], ODD_PAD), INPROJ_CHUNK)
            qkv = dn_conv(proj, dn_conv_w[o])
            gates_t = proj[:, :, O_A:O_A + 16].transpose(0, 2, 1)
            of = gdn_direction(qkv, proj, gates_t, dn_a_log[o], dn_dt_bias[o], False)
            ob = gdn_direction(qkv, proj, gates_t, dn_a_log[o], dn_dt_bias[o], True)
            mixer_in, gain, w_out = (of, ob, proj), dn_out_gain[o], od_w_out[o]
        x_new, hm, gate, top_idx = out_projection(layer % 2 == 0, mixer_in, gain, xs, mod, norm2_g[layer], w_out,
                                                  router_w[layer], router_b[layer])
        xs = moe_layer(x_new, hm, gate, top_idx, mod, layer, exp_w1, exp_b1[layer], exp_w2, exp_b2[layer])
    return xs[:, CTX_LEN:]
```
